```python
import math
import jax, jax.numpy as jnp
from jax import lax
import numpy as np

D_MODEL = 1024
BATCH = 4
SEQ = 8192
DEPTH = 4

CTX_LEN = 256
GRID_W = 64
N_MIXERS = 4
MIX_GDN, MIX_RET, MIX_GLA, MIX_HYENA = 0, 1, 2, 3
CHUNK = 64
EPS = 1e-6
D_FF = -(-8 * D_MODEL // (3 * 256)) * 256

H_A = D_MODEL // 128
DK_A = 128
DV_A = 128
QK_A = H_A * DK_A
V_A = H_A * DV_A
GDN_QKV = 2 * QK_A + V_A
GDN_CONV = 5
GDN_IN = GDN_QKV + V_A + 4 * H_A

H_R = D_MODEL // 256
DK_R = 256
DV_R = 512
QK_R = H_R * DK_R
V_R = H_R * DV_R
RET_IN = 2 * QK_R + 2 * V_R
ROPE_BASE = 10000.0

H_C = 4
DK_C = D_MODEL // 2 // H_C
DV_C = D_MODEL // H_C
QK_C = H_C * DK_C
V_C = H_C * DV_C
GLA_RANK = 16
GLA_GATE_NORM = 16.0
GLA_IN = 2 * QK_C + 2 * V_C + 2 * GLA_RANK

HY_ORDER = 2
HY_CONV = 3
HY_EMB = 33
HY_FF = 64
HY_TARGET = 1e-2
HY_FAST = 0.3
HY_SLOW = 1.5

kernel_name = 'hybrid_interleaved_diffusion_trunk'


def rms_norm(x, g):
    xf = x.astype(jnp.float32)
    y = xf * lax.rsqrt(jnp.mean(xf * xf, axis=-1, keepdims=True) + EPS)
    return (y * g.astype(jnp.float32)).astype(x.dtype)


def modulate(x, g, shift, scale):
    return rms_norm(x, g) * (1.0 + scale) + shift


def swiglu(h, w1, w3, w2):
    return (jax.nn.silu(h @ w1) * (h @ w3)) @ w2


def short_conv(u, w):
    k = w.shape[0]
    return lax.conv_general_dilated(u, w[:, None, :].astype(u.dtype), window_strides=(1,),
                                    padding=[(k // 2, k // 2)],
                                    dimension_numbers=('NWC', 'WIO', 'NWC'),
                                    feature_group_count=u.shape[-1])


def to_heads(t, n_heads):
    b, l, _ = t.shape
    return t.reshape(b, l, n_heads, -1).transpose(0, 2, 1, 3)


def from_heads(t):
    b, h, l, d = t.shape
    return t.transpose(0, 2, 1, 3).reshape(b, l, h * d)


def l2norm(t):
    tf = t.astype(jnp.float32)
    return tf * lax.rsqrt(jnp.sum(tf * tf, axis=-1, keepdims=True) + EPS)


def head_norm_gate(o, g, gate, center):
    if center:
        o = o - jnp.mean(o, axis=-1, keepdims=True)
    o = o * lax.rsqrt(jnp.mean(o * o, axis=-1, keepdims=True) + EPS)
    g = g.astype(jnp.float32)
    o = o * (g[:, None, :] if g.ndim == 2 else g)
    return from_heads(o).astype(gate.dtype) * jax.nn.silu(gate)


def axial_rotary(n, dim):
    rows = n // GRID_W
    row = jnp.repeat(jnp.arange(rows, dtype=jnp.float32), GRID_W)
    col = jnp.tile(jnp.arange(GRID_W, dtype=jnp.float32), rows)
    nf = dim // 4
    inv = ROPE_BASE ** (-jnp.arange(nf, dtype=jnp.float32) / nf)
    ang = jnp.concatenate([row[:, None] * inv, col[:, None] * inv], axis=-1)
    return jnp.cos(ang), jnp.sin(ang)


def apply_rotary(x, cos, sin):
    half = x.shape[-1] // 2
    x1, x2 = x[..., :half], x[..., half:]
    return jnp.concatenate([x1 * cos - x2 * sin, x1 * sin + x2 * cos], axis=-1)


def gated_delta_chunks(q, k, v, beta, g, s0):
    f32 = jnp.float32
    b, h, l, dk = q.shape
    dv = v.shape[-1]
    n = l // CHUNK
    q = q.astype(f32).reshape(b, h, n, CHUNK, dk)
    k = k.astype(f32).reshape(b, h, n, CHUNK, dk)
    v = v.astype(f32).reshape(b, h, n, CHUNK, dv)
    beta = beta.astype(f32).reshape(b, h, n, CHUNK)
    gc = jnp.cumsum(g.astype(f32).reshape(b, h, n, CHUNK), axis=-1)
    incl = jnp.tril(jnp.ones((CHUNK, CHUNK), bool))
    diff = gc[..., :, None] - gc[..., None, :]
    decay = jnp.where(incl, jnp.exp(jnp.where(incl, diff, 0.0)), 0.0)
    kb = k * beta[..., None]
    m = jnp.tril(jnp.einsum('bhnid,bhnjd->bhnij', kb, k) * decay, -1)
    a = jnp.eye(CHUNK, dtype=f32) + m
    rhs = jnp.concatenate([v * beta[..., None], kb * jnp.exp(gc)[..., None]], axis=-1)
    sol = lax.linalg.triangular_solve(a, rhs, left_side=True, lower=True, unit_diagonal=True)
    u, w = sol[..., :dv], sol[..., dv:]
    attn = jnp.einsum('bhnid,bhnjd->bhnij', q, k) * decay
    g_last = gc[..., -1:]
    xs = (u, w, attn, q * jnp.exp(gc)[..., None], k * jnp.exp(g_last - gc)[..., None],
          jnp.exp(g_last)[..., None])
    xs = tuple(jnp.moveaxis(t, 2, 0) for t in xs)

    def step(s, inp):
        u_n, w_n, attn_n, q_n, k_n, d_n = inp
        v_new = u_n - jnp.einsum('bhcd,bhde->bhce', w_n, s)
        o_n = jnp.einsum('bhcd,bhde->bhce', q_n, s) + jnp.einsum('bhij,bhje->bhie', attn_n, v_new)
        s = s * d_n + jnp.einsum('bhcd,bhce->bhde', k_n, v_new)
        return s, o_n

    s, o = lax.scan(step, s0.astype(f32), xs)
    return jnp.moveaxis(o, 0, 2).reshape(b, h, l, dv), s


def gla_chunks(q, k, v, logdecay, s0):
    f32 = jnp.float32
    b, h, l, dk = q.shape
    dv = v.shape[-1]
    n = l // CHUNK
    q = q.astype(f32).reshape(b, h, n, CHUNK, dk)
    k = k.astype(f32).reshape(b, h, n, CHUNK, dk)
    v = v.astype(f32).reshape(b, h, n, CHUNK, dv)
    ld = jnp.broadcast_to(logdecay.astype(f32), (b, h, l, dk)).reshape(b, h, n, CHUNK, dk)
    cum = jnp.cumsum(ld, axis=3)
    ref = cum[:, :, :, CHUNK // 2:CHUNK // 2 + 1]
    scores = jnp.einsum('bhnid,bhnjd->bhnij', q * jnp.exp(cum - ref), k * jnp.exp(ref - cum))
    incl = jnp.tril(jnp.ones((CHUNK, CHUNK), bool))
    o_intra = jnp.einsum('bhnij,bhnje->bhnie', jnp.where(incl, scores, 0.0), v)
    c_last = cum[:, :, :, -1:]
    xs = (q * jnp.exp(cum), k * jnp.exp(c_last - cum), v, jnp.swapaxes(jnp.exp(c_last), -1, -2))
    xs = tuple(jnp.moveaxis(t, 2, 0) for t in xs)

    def step(s, inp):
        q_n, k_n, v_n, d_n = inp
        o_n = jnp.einsum('bhcd,bhde->bhce', q_n, s)
        s = s * d_n + jnp.einsum('bhcd,bhce->bhde', k_n, v_n)
        return s, o_n

    s, o_inter = lax.scan(step, s0.astype(f32), xs)
    o = o_intra + jnp.moveaxis(o_inter, 0, 2)
    return o.reshape(b, h, l, dv), s


def bidirectional(scan_fn, ctx_dirs, lat_dirs, s0):
    o_lat, o_ctx = 0.0, 0.0
    for d in range(2):
        ca, la = ctx_dirs[d], lat_dirs[d]
        if d == 1:
            ca = [jnp.flip(t, axis=2) for t in ca]
            la = [jnp.flip(t, axis=2) for t in la]
        oc, sc = scan_fn(*ca, s0)
        ol, _ = scan_fn(*la, sc)
        if d == 1:
            oc, ol = jnp.flip(oc, axis=2), jnp.flip(ol, axis=2)
        o_lat = o_lat + ol
        o_ctx = o_ctx + oc
    return o_lat, o_ctx


def gdn_mixer(h, hc, w_in, conv_w, a_log, dt_bias, norm_g, w_out, ctx_out):
    def prep(t):
        bsz, l, _ = t.shape
        p = t @ w_in
        qkv = jax.nn.silu(short_conv(p[..., :GDN_QKV], conv_w))
        q = l2norm(to_heads(qkv[..., :QK_A], H_A)) * DK_A ** -0.5
        k = l2norm(to_heads(qkv[..., QK_A:2 * QK_A], H_A))
        v = to_heads(qkv[..., 2 * QK_A:], H_A).astype(jnp.float32)
        gate = p[..., GDN_QKV:GDN_QKV + V_A]
        sc = p[..., GDN_QKV + V_A:].astype(jnp.float32).reshape(bsz, l, 2, 2, H_A)
        sc = jnp.transpose(sc, (2, 3, 0, 4, 1))
        beta = jax.nn.sigmoid(sc[0])
        g = (-jnp.exp(a_log.astype(jnp.float32))[:, None, :, None]
             * jax.nn.softplus(sc[1] + dt_bias.astype(jnp.float32)[:, None, :, None]))
        return [(q, k, v, beta[d], g[d]) for d in range(2)], gate

    lat_dirs, gate = prep(h)
    ctx_dirs, gate_c = prep(hc)
    s0 = jnp.zeros((h.shape[0], H_A, DK_A, DV_A), jnp.float32)
    o_l, o_c = bidirectional(gated_delta_chunks, ctx_dirs, lat_dirs, s0)
    y = head_norm_gate(o_l, norm_g, gate, False) @ w_out
    yc = head_norm_gate(o_c, norm_g, gate_c, False) @ w_out if ctx_out else None
    return y, yc


def retention_mixer(h, hc, w_in, norm_g, w_out, ctx_out):
    log_gamma = jnp.log1p(-jnp.exp2(-5.0 - jnp.arange(H_R, dtype=jnp.float32)))
    decay_dirs = (log_gamma.reshape(1, H_R, 1, 1), log_gamma[::-1].reshape(1, H_R, 1, 1))

    def prep(t, rotary):
        p = t @ w_in
        q = to_heads(p[..., :QK_R], H_R).astype(jnp.float32) * DK_R ** -0.5
        k = to_heads(p[..., QK_R:2 * QK_R], H_R).astype(jnp.float32)
        if rotary:
            cos, sin = axial_rotary(t.shape[1], DK_R)
            q, k = apply_rotary(q, cos, sin), apply_rotary(k, cos, sin)
        v = to_heads(p[..., 2 * QK_R:2 * QK_R + V_R], H_R).astype(jnp.float32)
        return [(q, k, v, ld) for ld in decay_dirs], p[..., 2 * QK_R + V_R:]

    lat_dirs, gate = prep(h, True)
    ctx_dirs, gate_c = prep(hc, False)
    s0 = jnp.zeros((h.shape[0], H_R, DK_R, DV_R), jnp.float32)
    o_l, o_c = bidirectional(gla_chunks, ctx_dirs, lat_dirs, s0)
    y = head_norm_gate(o_l, norm_g, gate, True) @ w_out
    yc = head_norm_gate(o_c, norm_g, gate_c, True) @ w_out if ctx_out else None
    return y, yc


def gla_mixer(h, hc, w_in, gate_w2, gate_b, norm_g, w_out, ctx_out):
    def prep(t):
        bsz, l, _ = t.shape
        p = t @ w_in
        q = to_heads(p[..., :QK_C], H_C).astype(jnp.float32) * DK_C ** -0.5
        k = to_heads(p[..., QK_C:2 * QK_C], H_C).astype(jnp.float32)
        v = to_heads(p[..., 2 * QK_C:2 * QK_C + V_C], H_C).astype(jnp.float32)
        gate = p[..., 2 * QK_C + V_C:2 * QK_C + 2 * V_C]
        low = p[..., 2 * QK_C + 2 * V_C:].reshape(bsz, l, 2, GLA_RANK)
        logit = jnp.einsum('blyr,yrk->yblk', low, gate_w2) + gate_b[:, None, None, :]
        ld = jax.nn.log_sigmoid(logit.astype(jnp.float32)) / GLA_GATE_NORM
        return [(q, k, v, to_heads(ld[d], H_C)) for d in range(2)], gate

    lat_dirs, gate = prep(h)
    ctx_dirs, gate_c = prep(hc)
    s0 = jnp.zeros((h.shape[0], H_C, DK_C, DV_C), jnp.float32)
    o_l, o_c = bidirectional(gla_chunks, ctx_dirs, lat_dirs, s0)
    y = head_norm_gate(o_l, norm_g, gate, False) @ w_out
    yc = head_norm_gate(o_c, norm_g, gate_c, False) @ w_out if ctx_out else None
    return y, yc


def hyena_filters(l, ff_w1, ff_b1, ff_w2, ff_b2, ff_w3, sin_freq):
    f32 = jnp.float32
    pos = jnp.arange(l, dtype=f32)
    t = pos / (l - 1)
    bands = (HY_EMB - 1) // 2
    freqs = jnp.linspace(1e-4, bands - 1, bands, dtype=f32)
    ang = (2.0 * math.pi / l) * pos[:, None] * freqs[None, :]
    z = jnp.concatenate([t[:, None], jnp.cos(ang), -jnp.sin(ang)], axis=-1)
    fr = sin_freq.astype(f32)
    hid = jnp.sin(fr * (z @ ff_w1.astype(f32) + ff_b1.astype(f32)))
    hid = jnp.sin(fr * (hid @ ff_w2.astype(f32) + ff_b2.astype(f32)))
    filt = (hid @ ff_w3.astype(f32)).reshape(l, HY_ORDER, D_MODEL)
    dist = jnp.abs(pos - l // 2) / (l // 2)
    deltas = jnp.abs(jnp.linspace(math.log(HY_TARGET) / HY_SLOW, math.log(HY_TARGET) / HY_FAST,
                                  D_MODEL, dtype=f32))
    filt = filt * jnp.exp(-dist[:, None, None] * deltas)
    return filt / jnp.sum(jnp.abs(filt), axis=0, keepdims=True)


def fft_conv_centred(u, filt, bias):
    l = u.shape[1]
    n = 2 * l
    uf = u.astype(jnp.float32)
    y = jnp.fft.irfft(jnp.fft.rfft(uf, n=n, axis=1) * jnp.fft.rfft(filt, n=n, axis=0)[None], n=n, axis=1)
    y = y[:, l // 2:l // 2 + l]
    return (y + uf * bias.astype(jnp.float32)).astype(u.dtype)


def hyena_mixer(h, hc, w_in, conv_w, ff_w1, ff_b1, ff_w2, ff_b2, ff_w3, sin_freq, skip, w_out):
    def operator(t):
        l = t.shape[1]
        u = short_conv(t @ w_in, conv_w)
        v, x1, x2 = jnp.split(u, 3, axis=-1)
        filt = hyena_filters(l, ff_w1, ff_b1, ff_w2, ff_b2, ff_w3, sin_freq)
        z = x1 * fft_conv_centred(v, filt[:, 0], skip[0])
        z = x2 * fft_conv_centred(z, filt[:, 1], skip[1])
        return z @ w_out

    y = operator(h)
    yc = operator(hc) if hc is not None else None
    return y, yc


def setup_inputs(seed: int = 0) -> dict:
    key = jax.random.key(seed)
    keys = iter(jax.random.split(key, 64))
    f32 = jnp.float32

    def normal(shape, scale):
        return jax.random.normal(next(keys), shape, f32) * scale

    def gain(shape):
        return 1.0 + normal(shape, 0.02)

    n_a, n_b, n_c, n_d = [len(range(m, DEPTH, N_MIXERS)) for m in range(N_MIXERS)]
    d = D_MODEL
    inp = {}
    inp['x'] = normal((BATCH, SEQ, d), 1.0)
    inp['c'] = normal((BATCH, d), 1.0)
    inp['ctx'] = normal((BATCH, CTX_LEN, d), 1.0)
    inp['c_ctx'] = normal((d,), 1.0)
    inp['ada_w'] = normal((DEPTH, d, 6 * d), 0.5 * d ** -0.5)
    inp['ada_b'] = normal((DEPTH, 6 * d), 0.02)
    inp['norm1_g'] = gain((DEPTH, d))
    inp['norm2_g'] = gain((DEPTH, d))
    inp['ffn_w1'] = normal((DEPTH, d, D_FF), d ** -0.5)
    inp['ffn_w3'] = normal((DEPTH, d, D_FF), d ** -0.5)
    inp['ffn_w2'] = normal((DEPTH, D_FF, d), D_FF ** -0.5)
    inp['gdn_w_in'] = normal((n_a, d, GDN_IN), d ** -0.5)
    inp['gdn_conv_w'] = normal((n_a, GDN_CONV, GDN_QKV), GDN_CONV ** -0.5)
    inp['gdn_a_log'] = jnp.log(jax.random.uniform(next(keys), (n_a, 2, H_A), f32, 1.0, 16.0))
    dt = jnp.exp(jax.random.uniform(next(keys), (n_a, 2, H_A), f32, math.log(1e-3), math.log(1e-1)))
    inp['gdn_dt_bias'] = dt + jnp.log(-jnp.expm1(-dt))
    inp['gdn_norm_g'] = gain((n_a, DV_A))
    inp['gdn_w_out'] = normal((n_a, V_A, d), V_A ** -0.5)
    inp['ret_w_in'] = normal((n_b, d, RET_IN), d ** -0.5)
    inp['ret_norm_g'] = gain((n_b, H_R, DV_R))
    inp['ret_w_out'] = normal((n_b, V_R, d), V_R ** -0.5)
    inp['gla_w_in'] = normal((n_c, d, GLA_IN), d ** -0.5)
    inp['gla_gate_w2'] = normal((n_c, 2, GLA_RANK, QK_C), GLA_RANK ** -0.5)
    inp['gla_gate_b'] = normal((n_c, 2, QK_C), 0.1)
    inp['gla_norm_g'] = gain((n_c, DV_C))
    inp['gla_w_out'] = normal((n_c, V_C, d), V_C ** -0.5)
    inp['hy_w_in'] = normal((n_d, d, 3 * d), d ** -0.5)
    inp['hy_conv_w'] = normal((n_d, HY_CONV, 3 * d), HY_CONV ** -0.5)
    inp['hy_ff_w1'] = normal((n_d, HY_EMB, HY_FF), HY_EMB ** -0.5)
    inp['hy_ff_b1'] = normal((n_d, HY_FF), 0.1)
    inp['hy_ff_w2'] = normal((n_d, HY_FF, HY_FF), HY_FF ** -0.5)
    inp['hy_ff_b2'] = normal((n_d, HY_FF), 0.1)
    inp['hy_ff_w3'] = normal((n_d, HY_FF, HY_ORDER * d), HY_FF ** -0.5)
    inp['hy_sin_freq'] = gain((n_d, HY_FF))
    inp['hy_skip'] = normal((n_d, HY_ORDER, d), 1.0)
    inp['hy_w_out'] = normal((n_d, d, d), d ** -0.5)
    inp['final_norm_g'] = gain((d,))
    return inp


def reference(x, c, ctx, c_ctx, ada_w, ada_b, norm1_g, norm2_g, ffn_w1, ffn_w3, ffn_w2,
              gdn_w_in, gdn_conv_w, gdn_a_log, gdn_dt_bias, gdn_norm_g, gdn_w_out,
              ret_w_in, ret_norm_g, ret_w_out,
              gla_w_in, gla_gate_w2, gla_gate_b, gla_norm_g, gla_w_out,
              hy_w_in, hy_conv_w, hy_ff_w1, hy_ff_b1, hy_ff_w2, hy_ff_b2, hy_ff_w3, hy_sin_freq,
              hy_skip, hy_w_out, final_norm_g):
    silu_c = jax.nn.silu(c)
    silu_cc = jax.nn.silu(c_ctx)
    for i in range(DEPTH):
        kind = i % N_MIXERS
        j = i // N_MIXERS
        carry_ctx = any((l % N_MIXERS) != MIX_HYENA for l in range(i + 1, DEPTH))
        use_ctx = carry_ctx or kind != MIX_HYENA
        mod = silu_c @ ada_w[i] + ada_b[i]
        sh1, sc1, gt1, sh2, sc2, gt2 = jnp.split(mod[:, None, :], 6, axis=-1)
        h = modulate(x, norm1_g[i], sh1, sc1)
        hc = None
        if use_ctx:
            csh1, csc1, cgt1, csh2, csc2, cgt2 = jnp.split(silu_cc @ ada_w[i] + ada_b[i], 6)
            hc = modulate(ctx, norm1_g[i], csh1, csc1)
        if kind == MIX_GDN:
            y, yc = gdn_mixer(h, hc, gdn_w_in[j], gdn_conv_w[j], gdn_a_log[j], gdn_dt_bias[j],
                              gdn_norm_g[j], gdn_w_out[j], carry_ctx)
        elif kind == MIX_RET:
            y, yc = retention_mixer(h, hc, ret_w_in[j], ret_norm_g[j], ret_w_out[j], carry_ctx)
        elif kind == MIX_GLA:
            y, yc = gla_mixer(h, hc, gla_w_in[j], gla_gate_w2[j], gla_gate_b[j], gla_norm_g[j],
                              gla_w_out[j], carry_ctx)
        else:
            y, yc = hyena_mixer(h, hc, hy_w_in[j], hy_conv_w[j], hy_ff_w1[j], hy_ff_b1[j], hy_ff_w2[j],
                                hy_ff_b2[j], hy_ff_w3[j], hy_sin_freq[j], hy_skip[j], hy_w_out[j])
        x = x + gt1 * y
        x = x + gt2 * swiglu(modulate(x, norm2_g[i], sh2, sc2), ffn_w1[i], ffn_w3[i], ffn_w2[i])
        if carry_ctx:
            ctx = ctx + cgt1 * yc
            ctx = ctx + cgt2 * swiglu(modulate(ctx, norm2_g[i], csh2, csc2), ffn_w1[i], ffn_w3[i], ffn_w2[i])
    return rms_norm(x, final_norm_g)
```

```python
import functools
import math

import numpy as np
import jax
import jax.numpy as jnp
from jax import lax
from jax.experimental import pallas as pl
from jax.experimental.pallas import tpu as pltpu

F32 = jnp.float32
BF16 = jnp.bfloat16
EPS = 1e-6
CHUNK = 64
N_MIXERS = 4
GRID_W = 64
ROPE_BASE = 10000.0
GLA_GATE_NORM = 16.0
HY_TARGET, HY_FAST, HY_SLOW = 1e-2, 0.3, 1.5
LANES = 128
SUBLANES = 8
VMEM_LIMIT = 56 * 1024 * 1024


def _params(*sem):
    return pltpu.CompilerParams(dimension_semantics=sem, vmem_limit_bytes=VMEM_LIMIT)


def _tile(total, target, quantum):
    assert total % quantum == 0, (total, quantum)
    best, t = quantum, quantum
    while t <= min(total, target):
        if total % t == 0:
            best = t
        t += quantum
    return best


def _dot(a, b):
    return jnp.dot(a, b, preferred_element_type=F32)


def _dot_nt(a, b):
    return lax.dot_general(a, b, (((1,), (1,)), ((), ())), preferred_element_type=F32)


def _dot_tn(a, b):
    return lax.dot_general(a, b, (((0,), (0,)), ((), ())), preferred_element_type=F32)


def _split2(x):
    hi = x.astype(BF16)
    lo = (x - hi.astype(F32)).astype(BF16)
    return hi, lo


def _split3(x):
    hi = x.astype(BF16)
    r = x - hi.astype(F32)
    mid = r.astype(BF16)
    lo = (r - mid.astype(F32)).astype(BF16)
    return hi, mid, lo


def _dot3(a, b, dot=_dot):
    ah, al = _split2(a)
    bh, bl = _split2(b)
    return dot(ah, bh) + (dot(ah, bl) + dot(al, bh))


def _dot01(m01, x, dot=_dot):
    h, m, l = _split3(x)
    return dot(m01, h) + (dot(m01, m) + dot(m01, l))


def _dot01_tn(x, m01):
    h, m, l = _split3(x)
    return _dot_tn(h, m01) + (_dot_tn(m, m01) + _dot_tn(l, m01))


def _silu(x):
    return x * jax.nn.sigmoid(x)


def _softplus(x):
    return jnp.maximum(x, 0.0) + jnp.log(1.0 + jnp.exp(-jnp.abs(x)))


def _rms(x):
    return x * lax.rsqrt(jnp.mean(x * x, axis=-1, keepdims=True) + EPS)


def _row_select(rows_ctx, a_ctx, a_lat, shape):
    return jnp.where(jnp.broadcast_to(rows_ctx, shape), jnp.broadcast_to(a_ctx, shape),
                     jnp.broadcast_to(a_lat, shape))


def _is_ctx_rows(t, tm, lc):
    rows = t * tm + lax.broadcasted_iota(jnp.int32, (tm, 1), 0)
    return rows < lc


def _norm_mod(x, g, mb, mc, is_ctx, k):
    y = _rms(x) * g
    shift = _row_select(is_ctx, mc[k:k + 1], mb[k:k + 1], x.shape)
    scale = _row_select(is_ctx, mc[k + 1:k + 2], mb[k + 1:k + 2], x.shape)
    return y * (1.0 + scale) + shift


def _tri_masks(c, reverse):
    r = lax.broadcasted_iota(jnp.int32, (c, c), 0)
    q = lax.broadcasted_iota(jnp.int32, (c, c), 1)
    if reverse:
        return r <= q, r < q, r == q
    return r >= q, r > q, r == q


def _as01(mask):
    return jnp.where(mask, 1.0, 0.0).astype(BF16)


def _adaln_kernel(c_ref, w_ref, b_ref, o_ref):
    s = _silu(c_ref[...])
    o_ref[0] = _dot3(s, w_ref[0]) + b_ref[0]


def _adaln(cvec, ada_w, ada_b):
    depth, d, n = ada_w.shape
    tn = _tile(n, 1536, LANES)
    return pl.pallas_call(
        _adaln_kernel,
        grid=(depth, n // tn),
        in_specs=[pl.BlockSpec((SUBLANES, d), lambda i, j: (0, 0)),
                  pl.BlockSpec((1, d, tn), lambda i, j: (i, 0, j)),
                  pl.BlockSpec((1, 1, tn), lambda i, j: (i, 0, j))],
        out_specs=pl.BlockSpec((1, SUBLANES, tn), lambda i, j: (i, 0, j)),
        out_shape=jax.ShapeDtypeStruct((depth, SUBLANES, n), F32),
        compiler_params=_params("parallel", "parallel"),
        name="adaln",
    )(cvec, ada_w, ada_b.reshape(depth, 1, n))


def _in_proj_kernel(x_ref, g_ref, mb_ref, mc_ref, w_ref, o_ref, h_scr, *, tm, lc):
    @pl.when(pl.program_id(2) == 0)
    def _():
        is_ctx = _is_ctx_rows(pl.program_id(1), tm, lc)
        h = _norm_mod(x_ref[0], g_ref[...], mb_ref[0], mc_ref[0], is_ctx, 0)
        h_scr[...] = h.astype(BF16)

    o_ref[0] = _dot(h_scr[...], w_ref[...])


def _in_proj(xa, modl, g, w, lc, tn):
    b, lt, d = xa.shape
    n = w.shape[1]
    tm = _tile(lt, 768, LANES)
    return pl.pallas_call(
        functools.partial(_in_proj_kernel, tm=tm, lc=lc),
        grid=(b, lt // tm, n // tn),
        in_specs=[pl.BlockSpec((1, tm, d), lambda i, t, j: (i, t, 0)),
                  pl.BlockSpec((1, d), lambda i, t, j: (0, 0)),
                  pl.BlockSpec((1, 6, d), lambda i, t, j: (i, 0, 0)),
                  pl.BlockSpec((1, 6, d), lambda i, t, j: (b, 0, 0)),
                  pl.BlockSpec((d, tn), lambda i, t, j: (0, j))],
        out_specs=pl.BlockSpec((1, tm, tn), lambda i, t, j: (i, t, j)),
        out_shape=jax.ShapeDtypeStruct((b, lt, n), F32),
        scratch_shapes=[pltpu.VMEM((tm, d), BF16)],
        compiler_params=_params("parallel", "parallel", "arbitrary"),
        name="in_proj",
    )(xa, g.reshape(1, d), modl, modl, w)


def _out_proj_kernel(x_ref, of_ref, ob_ref, gate_ref, ng_ref, w_ref, mb_ref, mc_ref, o_ref,
                     *, heads, dv, center, tm, lc):
    o = of_ref[0] + ob_ref[0]
    parts = []
    for h in range(heads):
        oh = o[:, h * dv:(h + 1) * dv]
        if center:
            oh = oh - jnp.mean(oh, axis=-1, keepdims=True)
        parts.append(_rms(oh))
    on = jnp.concatenate(parts, axis=-1) * ng_ref[...]
    y = (on * _silu(gate_ref[0])).astype(BF16)
    yo = _dot(y, w_ref[...])
    is_ctx = _is_ctx_rows(pl.program_id(1), tm, lc)
    gt = _row_select(is_ctx, mc_ref[0][2:3], mb_ref[0][2:3], yo.shape)
    o_ref[0] = x_ref[0] + gt * yo


def _out_proj(xa, of, ob, p, gate_blk, ng, w, modl, lc, heads, dv, center):
    b, lt, d = xa.shape
    v = heads * dv
    tm = _tile(lt, 256, LANES)
    return pl.pallas_call(
        functools.partial(_out_proj_kernel, heads=heads, dv=dv, center=center, tm=tm, lc=lc),
        grid=(b, lt // tm),
        in_specs=[pl.BlockSpec((1, tm, d), lambda i, t: (i, t, 0)),
                  pl.BlockSpec((1, tm, v), lambda i, t: (i, t, 0)),
                  pl.BlockSpec((1, tm, v), lambda i, t: (i, t, 0)),
                  pl.BlockSpec((1, tm, v), lambda i, t: (i, t, gate_blk)),
                  pl.BlockSpec((1, v), lambda i, t: (0, 0)),
                  pl.BlockSpec((v, d), lambda i, t: (0, 0)),
                  pl.BlockSpec((1, 6, d), lambda i, t: (i, 0, 0)),
                  pl.BlockSpec((1, 6, d), lambda i, t: (b, 0, 0))],
        out_specs=pl.BlockSpec((1, tm, d), lambda i, t: (i, t, 0)),
        out_shape=jax.ShapeDtypeStruct((b, lt, d), F32),
        compiler_params=_params("parallel", "parallel"),
        name="out_proj",
    )(xa, of, ob, p, ng.reshape(1, v), w, modl, modl)


def _plain_out_kernel(x_ref, z_ref, w_ref, mb_ref, o_ref):
    yo = _dot(z_ref[0].astype(BF16), w_ref[...])
    o_ref[0] = x_ref[0] + mb_ref[0][2:3] * yo


def _plain_out(xa, z, w, modl, lc):
    b, lt, d = xa.shape
    l = z.shape[1]
    tm = _tile(math.gcd(l, lc), 256, LANES)
    off = lc // tm
    return pl.pallas_call(
        _plain_out_kernel,
        grid=(b, l // tm),
        in_specs=[pl.BlockSpec((1, tm, d), lambda i, t: (i, t + off, 0)),
                  pl.BlockSpec((1, tm, d), lambda i, t: (i, t, 0)),
                  pl.BlockSpec((d, d), lambda i, t: (0, 0)),
                  pl.BlockSpec((1, 6, d), lambda i, t: (i, 0, 0))],
        out_specs=pl.BlockSpec((1, tm, d), lambda i, t: (i, t + off, 0)),
        out_shape=jax.ShapeDtypeStruct((b, lt, d), F32),
        input_output_aliases={0: 0},
        compiler_params=_params("parallel", "parallel"),
        name="hyena_out",
    )(xa, z, w, modl)


def _ffn_kernel(x_ref, g_ref, mb_ref, mc_ref, w1_ref, w3_ref, w2_ref, o_ref, h_scr, acc_scr,
                *, tm, lc, nj):
    j = pl.program_id(2)
    is_ctx = _is_ctx_rows(pl.program_id(1), tm, lc)

    @pl.when(j == 0)
    def _():
        h = _norm_mod(x_ref[0], g_ref[...], mb_ref[0], mc_ref[0], is_ctx, 3)
        h_scr[...] = h.astype(BF16)

    h = h_scr[...]
    t = (_silu(_dot(h, w1_ref[...])) * _dot(h, w3_ref[...])).astype(BF16)
    part = _dot(t, w2_ref[...])

    @pl.when(j == 0)
    def _():
        acc_scr[...] = part

    @pl.when(j > 0)
    def _():
        acc_scr[...] += part

    @pl.when(j == nj - 1)
    def _():
        gt = _row_select(is_ctx, mc_ref[0][5:6], mb_ref[0][5:6], acc_scr.shape)
        o_ref[0] = x_ref[0] + gt * acc_scr[...]


def _ffn(xa, modl, g, w1, w3, w2, lc):
    b, lt, d = xa.shape
    ff = w1.shape[1]
    tm = _tile(lt, 768, LANES)
    tf = _tile(ff, 256, LANES)
    nj = ff // tf
    return pl.pallas_call(
        functools.partial(_ffn_kernel, tm=tm, lc=lc, nj=nj),
        grid=(b, lt // tm, nj),
        in_specs=[pl.BlockSpec((1, tm, d), lambda i, t, j: (i, t, 0)),
                  pl.BlockSpec((1, d), lambda i, t, j: (0, 0)),
                  pl.BlockSpec((1, 6, d), lambda i, t, j: (i, 0, 0)),
                  pl.BlockSpec((1, 6, d), lambda i, t, j: (b, 0, 0)),
                  pl.BlockSpec((d, tf), lambda i, t, j: (0, j)),
                  pl.BlockSpec((d, tf), lambda i, t, j: (0, j)),
                  pl.BlockSpec((tf, d), lambda i, t, j: (j, 0))],
        out_specs=pl.BlockSpec((1, tm, d), lambda i, t, j: (i, t, 0)),
        out_shape=jax.ShapeDtypeStruct((b, lt, d), F32),
        scratch_shapes=[pltpu.VMEM((tm, d), BF16), pltpu.VMEM((tm, d), F32)],
        compiler_params=_params("parallel", "parallel", "arbitrary"),
        name="ffn",
    )(xa, g.reshape(1, d), modl, modl, w1, w3, w2)


def _final_norm_kernel(x_ref, g_ref, o_ref):
    o_ref[0] = _rms(x_ref[0]) * g_ref[...]


def _final_norm(xa, g, lc):
    b, lt, d = xa.shape
    l = lt - lc
    tm = _tile(math.gcd(l, lc), 512, LANES)
    off = lc // tm
    return pl.pallas_call(
        _final_norm_kernel,
        grid=(b, l // tm),
        in_specs=[pl.BlockSpec((1, tm, d), lambda i, t: (i, t + off, 0)),
                  pl.BlockSpec((1, d), lambda i, t: (0, 0))],
        out_specs=pl.BlockSpec((1, tm, d), lambda i, t: (i, t, 0)),
        out_shape=jax.ShapeDtypeStruct((b, l, d), F32),
        compiler_params=_params("parallel", "parallel"),
        name="final_norm",
    )(xa, g.reshape(1, d))


def _conv_rows(prev8, cur, next8, w, taps, row0, lc, lt):
    tr = cur.shape[0]
    win = jnp.concatenate([prev8, cur, next8], axis=0)
    rows = row0 + lax.broadcasted_iota(jnp.int32, (tr, 1), 0)
    is_ctx = rows < lc
    lo = jnp.where(is_ctx, 0, lc)
    hi = jnp.where(is_ctx, lc, lt)
    acc = None
    for j in range(taps):
        s = j - taps // 2
        sl = win[SUBLANES + s:SUBLANES + s + tr]
        if s != 0:
            src = rows + s
            ok = jnp.logical_and(src >= lo, src < hi)
            sl = jnp.where(jnp.broadcast_to(ok, sl.shape), sl, 0.0)
        term = sl * w[j:j + 1]
        acc = term if acc is None else acc + term
    return acc


def _halo_specs(tr, width, col_fn, row_off, n_rows):
    r8 = tr // SUBLANES
    last8 = n_rows // SUBLANES - 1

    def prev_map(i, t, c):
        return (i, jnp.maximum((t + row_off) * r8 - 1, 0), col_fn(c))

    def cur_map(i, t, c):
        return (i, t + row_off, col_fn(c))

    def next_map(i, t, c):
        return (i, jnp.minimum((t + row_off + 1) * r8, last8), col_fn(c))

    return [pl.BlockSpec((1, SUBLANES, width), prev_map),
            pl.BlockSpec((1, tr, width), cur_map),
            pl.BlockSpec((1, SUBLANES, width), next_map)]


def _gdn_prep_kernel(pp_ref, pc_ref, pn_ref, w_ref, o_ref, *, tr, lc, lt, taps, dk):
    part = pl.program_id(2)
    row0 = pl.program_id(1) * tr
    u = _silu(_conv_rows(pp_ref[0], pc_ref[0], pn_ref[0], w_ref[...], taps, row0, lc, lt))
    heads = u.shape[1] // dk
    normed = []
    for h in range(heads):
        uh = u[:, h * dk:(h + 1) * dk]
        normed.append(uh * lax.rsqrt(jnp.sum(uh * uh, axis=-1, keepdims=True) + EPS))
    un = jnp.concatenate(normed, axis=-1)
    scale = jnp.where(part == 0, dk ** -0.5, 1.0)
    o_ref[0, 0] = jnp.where(part == 2, u, un * scale)


def _gdn_prep(p, conv_w, lc, qk, dk):
    b, lt, _ = p.shape
    taps = conv_w.shape[0]
    tr = _tile(lt, 256, LANES)
    specs = _halo_specs(tr, qk, lambda c: c, 0, lt)
    return pl.pallas_call(
        functools.partial(_gdn_prep_kernel, tr=tr, lc=lc, lt=lt, taps=taps, dk=dk),
        grid=(b, lt // tr, 3),
        in_specs=specs + [pl.BlockSpec((taps, qk), lambda i, t, c: (0, c))],
        out_specs=pl.BlockSpec((1, 1, tr, qk), lambda i, t, c: (c, i, t, 0)),
        out_shape=jax.ShapeDtypeStruct((3, b, lt, qk), F32),
        compiler_params=_params("parallel", "parallel", "parallel"),
        name="gdn_prep",
    )(p, p, p, conv_w)


def _gdn_gates_kernel(p_ref, a_ref, dt_ref, o_ref, *, nb):
    x = p_ref[0]
    lane = lax.broadcasted_iota(jnp.int32, x.shape, 1)
    beta = jax.nn.sigmoid(x)
    g = -jnp.exp(a_ref[...]) * _softplus(x + dt_ref[...])
    o_ref[0] = jnp.where(lane < nb, beta, g)


def _gdn_gates(p, a_log, dt_bias, col_blk):
    b, lt, _ = p.shape
    nb = a_log.size
    pad = jnp.zeros((LANES,), F32)
    a_row = pad.at[nb:2 * nb].set(a_log.reshape(-1)).reshape(1, LANES)
    dt_row = pad.at[nb:2 * nb].set(dt_bias.reshape(-1)).reshape(1, LANES)
    tr = _tile(lt, 1024, LANES)
    return pl.pallas_call(
        functools.partial(_gdn_gates_kernel, nb=nb),
        grid=(b, lt // tr),
        in_specs=[pl.BlockSpec((1, tr, LANES), lambda i, t: (i, t, col_blk)),
                  pl.BlockSpec((1, LANES), lambda i, t: (0, 0)),
                  pl.BlockSpec((1, LANES), lambda i, t: (0, 0))],
        out_specs=pl.BlockSpec((1, tr, LANES), lambda i, t: (i, t, 0)),
        out_shape=jax.ShapeDtypeStruct((b, lt, LANES), F32),
        compiler_params=_params("parallel", "parallel"),
        name="gdn_gates",
    )(p, a_row, dt_row)


def _lane_column(x, idx):
    lane = lax.broadcasted_iota(jnp.int32, x.shape, 1)
    return jnp.sum(jnp.where(lane == idx, x, 0.0), axis=-1, keepdims=True)


def _gdn_chunk(q, k, v, beta, g, s, reverse):
    c, dk = q.shape
    incl, strict, eye = _tri_masks(c, reverse)
    gc = _dot01(_as01(incl), jnp.broadcast_to(g, (c, LANES)))
    gcs = gc[:, :c]
    gr = _dot01(jnp.ones((c, c), BF16), jnp.where(eye, gcs, 0.0))
    decay = jnp.where(incl, jnp.exp(jnp.where(incl, gcs - gr, 0.0)), 0.0)
    gcd = jnp.broadcast_to(gc[:, :1], (c, dk))
    eg = jnp.exp(gcd)
    kb = k * beta
    kh, qh = k.astype(BF16), q.astype(BF16)
    m = jnp.where(strict, _dot_nt(kb.astype(BF16), kh) * decay, 0.0)
    pows = [m]
    span = 2
    while span < c:
        pows.append(_dot3(pows[-1], pows[-1]))
        span *= 2
    sol = jnp.concatenate([v * beta, kb * eg], axis=-1)
    for pw in reversed(pows[1:]):
        sol = sol + _dot3(pw, sol)
    sol = sol - _dot3(m, sol)
    dv = v.shape[1]
    u, w = sol[:, :dv], sol[:, dv:]
    attn = _dot_nt(qh, kh) * decay
    g_last = gcd[0:1] if reverse else gcd[c - 1:c]
    qd = q * eg
    kd = k * jnp.exp(g_last - gcd)
    sh = s.astype(BF16)
    v_new = u - _dot(w.astype(BF16), sh)
    vh = v_new.astype(BF16)
    o = _dot(qd.astype(BF16), sh) + _dot(attn.astype(BF16), vh)
    s_new = s * jnp.exp(g_last[:, :1]) + _dot_tn(kd.astype(BF16), vh)
    return o, s_new


def _gdn_scan_kernel(qf_ref, kf_ref, vf_ref, bgf_ref, qb_ref, kb_ref, vb_ref, bgb_ref,
                     of_ref, ob_ref, s_scr, *, hb, dk, nheads):
    @pl.when(pl.program_id(2) == 0)
    def _():
        s_scr[...] = jnp.zeros_like(s_scr)

    hg = pl.program_id(1)
    for rev, (q_ref, k_ref, v_ref, bg_ref, o_ref) in enumerate(
            ((qf_ref, kf_ref, vf_ref, bgf_ref, of_ref), (qb_ref, kb_ref, vb_ref, bgb_ref, ob_ref))):
        bg = bg_ref[0]
        for j in range(hb):
            head = hg * hb + j
            beta = _lane_column(bg, rev * nheads + head)
            g = _lane_column(bg, (2 + rev) * nheads + head)
            cols = slice(j * dk, (j + 1) * dk)
            o, s_new = _gdn_chunk(q_ref[0, 0][:, cols], k_ref[0, 0][:, cols], v_ref[0, 0][:, cols],
                                  beta, g, s_scr[rev, j], bool(rev))
            o_ref[0, :, cols] = o
            s_scr[rev, j] = s_new


def _chunk_maps(ncc, nc):
    def fwd(n):
        return n

    def bwd(n):
        return jnp.where(n < ncc, ncc - 1 - n, nc - 1 - n + ncc)

    return fwd, bwd


def _gdn_scan(qkv, bg, lc, nheads, dk, hb):
    _, b, lt, _ = qkv.shape
    nc, ncc = lt // CHUNK, lc // CHUNK
    fwd, bwd = _chunk_maps(ncc, nc)
    w = hb * dk
    in_specs = []
    for cm in (fwd, bwd):
        for part in range(3):
            in_specs.append(pl.BlockSpec((1, 1, CHUNK, w), lambda i, h, n, cm=cm, part=part: (part, i, cm(n), h)))
        in_specs.append(pl.BlockSpec((1, CHUNK, LANES), lambda i, h, n, cm=cm: (i, cm(n), 0)))
    out_specs = [pl.BlockSpec((1, CHUNK, w), lambda i, h, n, cm=cm: (i, cm(n), h)) for cm in (fwd, bwd)]
    shape = jax.ShapeDtypeStruct((b, lt, nheads * dk), F32)
    return pl.pallas_call(
        functools.partial(_gdn_scan_kernel, hb=hb, dk=dk, nheads=nheads),
        grid=(b, nheads // hb, nc),
        in_specs=in_specs,
        out_specs=out_specs,
        out_shape=[shape, shape],
        scratch_shapes=[pltpu.VMEM((2, hb, dk, dk), F32)],
        compiler_params=_params("parallel", "parallel", "arbitrary"),
        name="gdn_scan",
    )(qkv, qkv, qkv, bg, qkv, qkv, qkv, bg)


def _lin_chunk(q, k, v, ld, s, reverse):
    c, dk = q.shape
    dv = v.shape[1]
    incl, _, _ = _tri_masks(c, reverse)
    cum = _dot01(_as01(incl), ld)
    mid = c // 2 - 1 if reverse else c // 2
    ref = cum[mid:mid + 1]
    qa = (q * jnp.exp(cum - ref)).astype(BF16)
    ka = (k * jnp.exp(ref - cum)).astype(BF16)
    scores = jnp.where(incl, _dot_nt(qa, ka), 0.0)
    vh = v.astype(BF16)
    c_last = cum[0:1] if reverse else cum[c - 1:c]
    qn = (q * jnp.exp(cum)).astype(BF16)
    kn = (k * jnp.exp(c_last - cum)).astype(BF16)
    o = _dot(scores.astype(BF16), vh) + _dot(qn, s.astype(BF16))
    tot = _dot01_tn(ld, jnp.ones((c, LANES), BF16))
    dn = jnp.broadcast_to(jnp.exp(tot[:, :1]), (dk, dv))
    s_new = s * dn + _dot_tn(kn, vh)
    return o, s_new


def _rotate(x, cos, sin):
    half = x.shape[1] // 2
    x1, x2 = x[:, :half], x[:, half:]
    return jnp.concatenate([x1 * cos - x2 * sin, x1 * sin + x2 * cos], axis=-1)


def _ret_scan_kernel(qf_ref, kf_ref, vf_ref, cf_ref, sf_ref, qb_ref, kb_ref, vb_ref, cb_ref, sb_ref,
                     ld_ref, of_ref, ob_ref, s_scr, *, dk):
    @pl.when(pl.program_id(2) == 0)
    def _():
        s_scr[...] = jnp.zeros_like(s_scr)

    for rev, (q_ref, k_ref, v_ref, c_ref, sn_ref, o_ref) in enumerate(
            ((qf_ref, kf_ref, vf_ref, cf_ref, sf_ref, of_ref), (qb_ref, kb_ref, vb_ref, cb_ref, sb_ref, ob_ref))):
        cos, sin = c_ref[...], sn_ref[...]
        q = _rotate(q_ref[0] * dk ** -0.5, cos, sin)
        k = _rotate(k_ref[0], cos, sin)
        ld = jnp.broadcast_to(ld_ref[rev, 0], (CHUNK, dk))
        o, s_new = _lin_chunk(q, k, v_ref[0], ld, s_scr[rev], bool(rev))
        o_ref[0] = o
        s_scr[rev] = s_new


def _gla_scan_kernel(qf_ref, kf_ref, vf_ref, lf_ref, qb_ref, kb_ref, vb_ref, lb_ref,
                     gw_ref, gb_ref, of_ref, ob_ref, s_scr, *, dk):
    @pl.when(pl.program_id(2) == 0)
    def _():
        s_scr[...] = jnp.zeros_like(s_scr)

    for rev, (q_ref, k_ref, v_ref, low_ref, o_ref) in enumerate(
            ((qf_ref, kf_ref, vf_ref, lf_ref, of_ref), (qb_ref, kb_ref, vb_ref, lb_ref, ob_ref))):
        logit = _dot(low_ref[0].astype(BF16), gw_ref[rev, 0]) + gb_ref[rev]
        ld = -_softplus(-logit) * (1.0 / GLA_GATE_NORM)
        o, s_new = _lin_chunk(q_ref[0] * dk ** -0.5, k_ref[0], v_ref[0], ld, s_scr[rev], bool(rev))
        o_ref[0] = o
        s_scr[rev] = s_new


def _lin_scan(kernel_fn, p, lc, heads, dk, dv, extra_chunk, extra_const, name):
    b, lt, _ = p.shape
    nc, ncc = lt // CHUNK, lc // CHUNK
    fwd, bwd = _chunk_maps(ncc, nc)
    k_off = heads
    v_off = 2 * heads * dk // dv
    in_specs, args = [], []
    for cm in (fwd, bwd):
        in_specs += [pl.BlockSpec((1, CHUNK, dk), lambda i, h, n, cm=cm: (i, cm(n), h)),
                     pl.BlockSpec((1, CHUNK, dk), lambda i, h, n, cm=cm: (i, cm(n), k_off + h)),
                     pl.BlockSpec((1, CHUNK, dv), lambda i, h, n, cm=cm: (i, cm(n), v_off + h))]
        args += [p, p, p]
        for arr, blk, fn in extra_chunk:
            in_specs.append(pl.BlockSpec(blk, lambda i, h, n, cm=cm, fn=fn: fn(cm(n), i, h)))
            args.append(arr)
    for arr, blk, fn in extra_const:
        in_specs.append(pl.BlockSpec(blk, lambda i, h, n, fn=fn: fn(i, h)))
        args.append(arr)
    out_specs = [pl.BlockSpec((1, CHUNK, dv), lambda i, h, n, cm=cm: (i, cm(n), h)) for cm in (fwd, bwd)]
    shape = jax.ShapeDtypeStruct((b, lt, heads * dv), F32)
    return pl.pallas_call(
        functools.partial(kernel_fn, dk=dk),
        grid=(b, heads, nc),
        in_specs=in_specs,
        out_specs=out_specs,
        out_shape=[shape, shape],
        scratch_shapes=[pltpu.VMEM((2, dk, dv), F32)],
        compiler_params=_params("parallel", "parallel", "arbitrary"),
        name=name,
    )(*args)


def _axial_tables(l, lc, dim):
    rows = l // GRID_W
    row = jnp.repeat(jnp.arange(rows, dtype=F32), GRID_W)
    col = jnp.tile(jnp.arange(GRID_W, dtype=F32), rows)
    nf = dim // 4
    inv = ROPE_BASE ** (-jnp.arange(nf, dtype=F32) / nf)
    ang = jnp.concatenate([row[:, None] * inv, col[:, None] * inv], axis=-1)
    ang = jnp.concatenate([jnp.zeros((lc, dim // 2), F32), ang], axis=0)
    return jnp.cos(ang), jnp.sin(ang)


def _hy_conv_kernel(pp_ref, pc_ref, pn_ref, w_ref, o_ref, *, tr, lc, lt, taps, row_off):
    row0 = (pl.program_id(1) + row_off) * tr
    o_ref[0, 0] = _conv_rows(pp_ref[0], pc_ref[0], pn_ref[0], w_ref[...], taps, row0, lc, lt)


def _hy_conv(p, conv_w, lc, d):
    b, lt, _ = p.shape
    l = lt - lc
    taps = conv_w.shape[0]
    tr = _tile(math.gcd(l, lc), 256, LANES)
    row_off = lc // tr
    specs = _halo_specs(tr, d, lambda c: c, row_off, lt)
    return pl.pallas_call(
        functools.partial(_hy_conv_kernel, tr=tr, lc=lc, lt=lt, taps=taps, row_off=row_off),
        grid=(b, l // tr, 3),
        in_specs=specs + [pl.BlockSpec((taps, d), lambda i, t, c: (0, c))],
        out_specs=pl.BlockSpec((1, 1, tr, d), lambda i, t, c: (c, i, t, 0)),
        out_shape=jax.ShapeDtypeStruct((3, b, l, d), F32),
        compiler_params=_params("parallel", "parallel", "parallel"),
        name="hyena_conv",
    )(p, p, p, conv_w)


def _hy_filter_kernel(z_ref, w1_ref, b1_ref, w2_ref, b2_ref, w3_ref, fr_ref, dl_ref, f_ref, s_ref, *, tr, l):
    t = pl.program_id(0)
    fr = fr_ref[...]
    hid = jnp.sin(fr * (_dot3(z_ref[...], w1_ref[...]) + b1_ref[...]))
    hid = jnp.sin(fr * (_dot3(hid, w2_ref[...]) + b2_ref[...]))
    filt = _dot3(hid, w3_ref[...])
    pos = (t * tr + lax.broadcasted_iota(jnp.int32, (tr, 1), 0)).astype(F32)
    dist = jnp.abs(pos - (l // 2)) / (l // 2)
    filt = filt * jnp.exp(-dist * dl_ref[...])
    f_ref[...] = filt
    part = jnp.sum(jnp.abs(filt), axis=0, keepdims=True)

    @pl.when(t == 0)
    def _():
        s_ref[...] = part

    @pl.when(t > 0)
    def _():
        s_ref[...] += part


def _hy_filter(l, w1, b1, w2, b2, w3, fr, d):
    emb, hf = w1.shape
    n_out = w3.shape[1]
    pos = jnp.arange(l, dtype=F32)
    tt = pos / (l - 1)
    bands = (emb - 1) // 2
    freqs = jnp.linspace(1e-4, bands - 1, bands, dtype=F32)
    ang = (2.0 * math.pi / l) * pos[:, None] * freqs[None, :]
    z = jnp.concatenate([tt[:, None], jnp.cos(ang), -jnp.sin(ang)], axis=-1)
    z = jnp.pad(z, ((0, 0), (0, LANES - emb)))
    w1p = jnp.pad(w1, ((0, LANES - emb), (0, 0)))
    deltas = jnp.abs(jnp.linspace(math.log(HY_TARGET) / HY_SLOW, math.log(HY_TARGET) / HY_FAST, d, dtype=F32))
    dl = jnp.tile(deltas, n_out // d).reshape(1, n_out)
    tr = _tile(l, 256, LANES)
    full = lambda shape: pl.BlockSpec(shape, lambda t: (0,) * len(shape))
    return pl.pallas_call(
        functools.partial(_hy_filter_kernel, tr=tr, l=l),
        grid=(l // tr,),
        in_specs=[pl.BlockSpec((tr, LANES), lambda t: (t, 0)),
                  full((LANES, hf)), full((1, hf)), full((hf, hf)), full((1, hf)), full((hf, n_out)),
                  full((1, hf)), full((1, n_out))],
        out_specs=[pl.BlockSpec((tr, n_out), lambda t: (t, 0)), full((1, n_out))],
        out_shape=[jax.ShapeDtypeStruct((l, n_out), F32), jax.ShapeDtypeStruct((1, n_out), F32)],
        compiler_params=_params("arbitrary"),
        name="hyena_filter",
    )(z, w1p, b1.reshape(1, hf), w2, b2.reshape(1, hf), w3, fr.reshape(1, hf), dl)


def _dft_consts(l):
    n = 2 * l
    n1 = math.isqrt(n)
    assert n1 * n1 == n and n1 % 4 == 0, n
    h = n1 // 2
    idx = np.arange(n1)
    ang = 2.0 * np.pi * np.outer(idx, idx[:h]) / n1
    c, s = np.cos(ang), np.sin(ang)
    fa = np.block([[c, s], [-s, c]])
    ang = 2.0 * np.pi * np.outer(idx[:h] + n1 // 4, idx) / n1
    c, s = np.cos(ang) / n, np.sin(ang) / n
    fc = np.block([[c, -s], [s, c]])
    return n1, jnp.asarray(fa, F32), jnp.asarray(fc, F32)


def _mid_tables(n1):
    n = n1 * n1
    k1 = jnp.arange(n1, dtype=jnp.int32)[:, None, None]
    a = jnp.arange(n1, dtype=jnp.int32)[None, :, None]
    bb = jnp.arange(n1, dtype=jnp.int32)[None, None, :]
    mf = (bb * (n1 * a + k1)) % n
    mi = (a * (n1 * bb + k1)) % n
    out = []
    for m, sign in ((mf, -1.0), (mi, 1.0)):
        ang = m.astype(F32) * (2.0 * math.pi / n)
        c, s = jnp.cos(ang), sign * jnp.sin(ang)
        top = jnp.concatenate([c, -s], axis=2)
        bot = jnp.concatenate([s, c], axis=2)
        out.append(jnp.concatenate([top, bot], axis=1))
    return out


def _fft_a_kernel(z_ref, fa_ref, o_ref, *, n1):
    z = z_ref[...]
    zz = z.reshape(z.shape[0] * z.shape[1], z.shape[2]).astype(BF16)
    out = _dot(fa_ref[...], zz)
    o_ref[0, 0] = out[:n1]
    o_ref[0, 1] = out[n1:]


def _fft_a(u, fa, n1, pairs):
    g, h, w = u.shape
    tl = _tile(w, 2048, LANES)
    fa_b = fa[:, :pairs * h].astype(BF16)
    return pl.pallas_call(
        functools.partial(_fft_a_kernel, n1=n1),
        grid=(g // pairs, w // tl),
        in_specs=[pl.BlockSpec((pairs, h, tl), lambda p, j: (p, 0, j)),
                  pl.BlockSpec((2 * n1, pairs * h), lambda p, j: (0, 0))],
        out_specs=pl.BlockSpec((1, 2, n1, tl), lambda p, j: (p, 0, 0, j)),
        out_shape=jax.ShapeDtypeStruct((g // pairs, 2, n1, w), F32),
        compiler_params=_params("parallel", "parallel"),
        name="fft_first",
    )(u, fa_b)


def _fft_spec_kernel(b_ref, tf_ref, s_ref, o_ref, *, n1):
    x = jnp.concatenate([b_ref[0, 0, 0], b_ref[0, 1, 0]], axis=0).astype(BF16)
    a = _dot(tf_ref[0], x) * (1.0 / s_ref[...])
    o_ref[0, 0] = a[:n1]
    o_ref[1, 0] = a[n1:]


def _fft_spec(bc, tf, norm, n1, dcols):
    bc5 = bc.reshape(1, 2, n1, n1, dcols)
    return pl.pallas_call(
        functools.partial(_fft_spec_kernel, n1=n1),
        grid=(n1,),
        in_specs=[pl.BlockSpec((1, 2, 1, n1, dcols), lambda k: (0, 0, k, 0, 0)),
                  pl.BlockSpec((1, 2 * n1, 2 * n1), lambda k: (k, 0, 0)),
                  pl.BlockSpec((1, dcols), lambda k: (0, 0))],
        out_specs=pl.BlockSpec((2, 1, n1, dcols), lambda k: (0, k, 0, 0)),
        out_shape=jax.ShapeDtypeStruct((2, n1, n1, dcols), F32),
        compiler_params=_params("parallel"),
        name="fft_filter_spectrum",
    )(bc5, tf, norm)


def _fft_mid_kernel(b_ref, tf_ref, ti_ref, f_ref, o_ref, *, n1):
    x = jnp.concatenate([b_ref[0, 0, 0], b_ref[0, 1, 0]], axis=0).astype(BF16)
    a = _dot(tf_ref[0], x)
    ar, ai = a[:n1], a[n1:]
    fr, fi = f_ref[0, 0], f_ref[1, 0]
    y = jnp.concatenate([ar * fr - ai * fi, ar * fi + ai * fr], axis=0).astype(BF16)
    gq = _dot(ti_ref[0], y)
    o_ref[0, 0, 0] = gq[:n1]
    o_ref[0, 1, 0] = gq[n1:]


def _fft_mid(bc, tf, ti, spec, order, n1, d):
    npair = bc.shape[0]
    bc5 = bc.reshape(npair, 2, n1, n1, d)
    return pl.pallas_call(
        functools.partial(_fft_mid_kernel, n1=n1),
        grid=(npair, n1),
        in_specs=[pl.BlockSpec((1, 2, 1, n1, d), lambda p, k: (p, 0, k, 0, 0)),
                  pl.BlockSpec((1, 2 * n1, 2 * n1), lambda p, k: (k, 0, 0)),
                  pl.BlockSpec((1, 2 * n1, 2 * n1), lambda p, k: (k, 0, 0)),
                  pl.BlockSpec((2, 1, n1, d), lambda p, k: (0, k, 0, order))],
        out_specs=pl.BlockSpec((1, 2, 1, n1, d), lambda p, k: (p, 0, k, 0, 0)),
        out_shape=jax.ShapeDtypeStruct((npair, 2, n1, n1, d), F32),
        compiler_params=_params("parallel", "parallel"),
        name="fft_mid",
    )(bc5, tf, ti, spec)


def _fft_c_kernel(g_ref, fc_ref, u_ref, x_ref, sk_ref, o_ref, *, h):
    x = jnp.concatenate([g_ref[0, 0], g_ref[0, 1]], axis=0).astype(BF16)
    y = _dot(fc_ref[...], x)
    reps = u_ref.shape[2] // sk_ref.shape[1]
    skip = jnp.concatenate([sk_ref[...]] * reps, axis=1)
    o_ref[0] = x_ref[0] * (y[:h] + u_ref[0] * skip)
    o_ref[1] = x_ref[1] * (y[h:] + u_ref[1] * skip)


def _fft_c(gc, fc, u, xg, skip, n1, d):
    npair = gc.shape[0]
    g, h, w = u.shape
    gc4 = gc.reshape(npair, 2, n1, w)
    tl = _tile(w, 2048, d)
    return pl.pallas_call(
        functools.partial(_fft_c_kernel, h=h),
        grid=(npair, w // tl),
        in_specs=[pl.BlockSpec((1, 2, n1, tl), lambda p, j: (p, 0, 0, j)),
                  pl.BlockSpec((2 * h, 2 * n1), lambda p, j: (0, 0)),
                  pl.BlockSpec((2, h, tl), lambda p, j: (p, 0, j)),
                  pl.BlockSpec((2, h, tl), lambda p, j: (p, 0, j)),
                  pl.BlockSpec((1, d), lambda p, j: (0, 0))],
        out_specs=pl.BlockSpec((2, h, tl), lambda p, j: (p, 0, j)),
        out_shape=jax.ShapeDtypeStruct((g, h, w), F32),
        compiler_params=_params("parallel", "parallel"),
        name="fft_last",
    )(gc4, fc.astype(BF16), u, xg, skip.reshape(1, d))


def _pad_cols(w, n):
    return jnp.pad(w, ((0, 0), (0, n - w.shape[1])))


def _gdn_layer(xa, modl, g1, lc, w_in, conv_w, a_log, dt_bias, norm_g, w_out):
    d = xa.shape[2]
    dk = 128
    heads = d // dk
    qk = heads * dk
    n_in = w_in.shape[1]
    n_pad = -(-n_in // 384) * 384
    p = _in_proj(xa, modl, g1, _pad_cols(w_in, n_pad).astype(BF16), lc, 384)
    qkv = _gdn_prep(p, conv_w, lc, qk, dk)
    bg = _gdn_gates(p, a_log, dt_bias, 4 * qk // LANES)
    of, ob = _gdn_scan(qkv, bg, lc, heads, dk, 2)
    return _out_proj(xa, of, ob, p, 3, jnp.tile(norm_g, heads), w_out.astype(BF16), modl, lc, heads, dk, False)


def _ret_layer(xa, modl, g1, lc, w_in, norm_g, w_out):
    d = xa.shape[2]
    heads, dk, dv = d // 256, 256, 512
    p = _in_proj(xa, modl, g1, w_in.astype(BF16), lc, 768)
    cos, sin = _axial_tables(xa.shape[1] - lc, lc, dk)
    log_gamma = jnp.log1p(-jnp.exp2(-5.0 - jnp.arange(heads, dtype=F32)))
    ld = jnp.stack([log_gamma, log_gamma[::-1]])
    ld = jnp.broadcast_to(ld[:, :, None, None], (2, heads, 1, dk))
    half = dk // 2
    tab = lambda arr: (arr, (CHUNK, half), lambda c, i, h: (c, 0))
    of, ob = _lin_scan(_ret_scan_kernel, p, lc, heads, dk, dv, [tab(cos), tab(sin)],
                       [(ld, (2, 1, 1, dk), lambda i, h: (0, h, 0, 0))], "ret_scan")
    return _out_proj(xa, of, ob, p, 2, norm_g.reshape(-1), w_out.astype(BF16), modl, lc, heads, dv, True)


def _gla_layer(xa, modl, g1, lc, w_in, gate_w2, gate_b, norm_g, w_out):
    d = xa.shape[2]
    heads = 4
    dk, dv = d // 2 // heads, d // heads
    rank = gate_w2.shape[1]
    n_in = w_in.shape[1]
    n_pad = -(-n_in // 640) * 640
    p = _in_proj(xa, modl, g1, _pad_cols(w_in, n_pad).astype(BF16), lc, 640)
    low_blk = (2 * heads * dk + 2 * heads * dv) // LANES
    gw = jnp.zeros((2, heads, LANES, dk), F32)
    for dr in range(2):
        blk = gate_w2[dr].reshape(rank, heads, dk).transpose(1, 0, 2)
        gw = gw.at[dr, :, dr * rank:(dr + 1) * rank, :].set(blk)
    gb = gate_b.reshape(2, 1, heads * dk)
    of, ob = _lin_scan(_gla_scan_kernel, p, lc, heads, dk, dv,
                       [(p, (1, CHUNK, LANES), lambda c, i, h: (i, c, low_blk))],
                       [(gw.astype(BF16), (2, 1, LANES, dk), lambda i, h: (0, h, 0, 0)),
                        (gb, (2, 1, dk), lambda i, h: (0, 0, h))], "gla_scan")
    return _out_proj(xa, of, ob, p, 2, jnp.tile(norm_g, heads), w_out.astype(BF16), modl, lc, heads, dv, False)


def _hyena_layer(xa, modl, g1, lc, w_in, conv_w, w1, b1, w2, b2, w3, fr, skip, w_out):
    b, lt, d = xa.shape
    l = lt - lc
    assert b % 2 == 0
    n1, fa, fc = _dft_consts(l)
    h = n1 // 2
    tf, ti = _mid_tables(n1)
    tf, ti = tf.astype(BF16), ti.astype(BF16)
    order = w3.shape[1] // d

    filt, norm = _hy_filter(l, w1, b1, w2, b2, w3, fr, d)
    fb = _fft_a(filt.reshape(1, h, n1 * order * d), fa, n1, 1)
    spec = _fft_spec(fb, tf, norm, n1, order * d)

    p = _in_proj(xa, modl, g1, w_in.astype(BF16), lc, 768)
    vx = _hy_conv(p, conv_w, lc, d).reshape(3, b, h, n1 * d)
    z = vx[0]
    for o in range(order):
        bc = _fft_a(z, fa, n1, 2)
        gq = _fft_mid(bc, tf, ti, spec, o, n1, d)
        z = _fft_c(gq, fc, z, vx[1 + o], skip[o], n1, d)
    return _plain_out(xa, z.reshape(b, l, d), w_out.astype(BF16), modl, lc)


def kernel(x, c, ctx, c_ctx, ada_w, ada_b, norm1_g, norm2_g, ffn_w1, ffn_w3, ffn_w2, gdn_w_in, gdn_conv_w, gdn_a_log, gdn_dt_bias, gdn_norm_g, gdn_w_out, ret_w_in, ret_norm_g, ret_w_out, gla_w_in, gla_gate_w2, gla_gate_b, gla_norm_g, gla_w_out, hy_w_in, hy_conv_w, hy_ff_w1, hy_ff_b1, hy_ff_w2, hy_ff_b2, hy_ff_w3, hy_sin_freq, hy_skip, hy_w_out, final_norm_g):
    b, l, d = x.shape
    lc = ctx.shape[1]
    depth = ada_w.shape[0]
    assert b < SUBLANES and lc % CHUNK == 0 and l % CHUNK == 0
    xa = jnp.concatenate([ctx, x], axis=1)
    cvec = jnp.zeros((SUBLANES, d), F32).at[:b].set(c).at[b].set(c_ctx)
    mod = _adaln(cvec, ada_w, ada_b).reshape(depth, SUBLANES, 6, d)
    for i in range(depth):
        kind, j = i % N_MIXERS, i // N_MIXERS
        modl, g1 = mod[i], norm1_g[i]
        if kind == 0:
            xa = _gdn_layer(xa, modl, g1, lc, gdn_w_in[j], gdn_conv_w[j], gdn_a_log[j], gdn_dt_bias[j],
                            gdn_norm_g[j], gdn_w_out[j])
        elif kind == 1:
            xa = _ret_layer(xa, modl, g1, lc, ret_w_in[j], ret_norm_g[j], ret_w_out[j])
        elif kind == 2:
            xa = _gla_layer(xa, modl, g1, lc, gla_w_in[j], gla_gate_w2[j], gla_gate_b[j], gla_norm_g[j],
                            gla_w_out[j])
        else:
            xa = _hyena_layer(xa, modl, g1, lc, hy_w_in[j], hy_conv_w[j], hy_ff_w1[j], hy_ff_b1[j], hy_ff_w2[j],
                              hy_ff_b2[j], hy_ff_w3[j], hy_sin_freq[j], hy_skip[j], hy_w_out[j])
        xa = _ffn(xa, modl, norm2_g[i], ffn_w1[i].astype(BF16), ffn_w3[i].astype(BF16), ffn_w2[i].astype(BF16), lc)
    return _final_norm(xa, final_norm_g, lc)
```

```python
import functools
import math

import numpy as np
import jax
import jax.numpy as jnp
from jax import lax
from jax.experimental import pallas as pl
from jax.experimental.pallas import tpu as pltpu

F32 = jnp.float32
BF16 = jnp.bfloat16
EPS = 1e-6
CHUNK = 64
N_MIXERS = 4
GRID_W = 64
ROPE_BASE = 10000.0
GLA_GATE_NORM = 16.0
HY_TARGET, HY_FAST, HY_SLOW = 1e-2, 0.3, 1.5
LANES = 128
SUBLANES = 8
VMEM_LIMIT = 56 * 1024 * 1024


def _params(*sem):
    return pltpu.CompilerParams(dimension_semantics=sem, vmem_limit_bytes=VMEM_LIMIT)


def _tile(total, target, quantum):
    assert total % quantum == 0, (total, quantum)
    best, t = quantum, quantum
    while t <= min(total, target):
        if total % t == 0:
            best = t
        t += quantum
    return best


def _dot(a, b):
    return jnp.dot(a, b, preferred_element_type=F32)


def _dot_nt(a, b):
    return lax.dot_general(a, b, (((1,), (1,)), ((), ())), preferred_element_type=F32)


def _dot_tn(a, b):
    return lax.dot_general(a, b, (((0,), (0,)), ((), ())), preferred_element_type=F32)


def _split2(x):
    hi = x.astype(BF16)
    lo = (x - hi.astype(F32)).astype(BF16)
    return hi, lo


def _split3(x):
    hi = x.astype(BF16)
    r = x - hi.astype(F32)
    mid = r.astype(BF16)
    lo = (r - mid.astype(F32)).astype(BF16)
    return hi, mid, lo


def _dot3(a, b, dot=_dot):
    ah, al = _split2(a)
    bh, bl = _split2(b)
    return dot(ah, bh) + (dot(ah, bl) + dot(al, bh))


def _dot01(m01, x, dot=_dot):
    h, m, l = _split3(x)
    return dot(m01, h) + (dot(m01, m) + dot(m01, l))


def _dot01_tn(x, m01):
    h, m, l = _split3(x)
    return _dot_tn(h, m01) + (_dot_tn(m, m01) + _dot_tn(l, m01))


def _silu(x):
    return x * jax.nn.sigmoid(x)


def _softplus(x):
    return jnp.maximum(x, 0.0) + jnp.log(1.0 + jnp.exp(-jnp.abs(x)))


def _rms(x):
    return x * lax.rsqrt(jnp.mean(x * x, axis=-1, keepdims=True) + EPS)


def _row_select(rows_ctx, a_ctx, a_lat, shape):
    return jnp.where(jnp.broadcast_to(rows_ctx, shape), jnp.broadcast_to(a_ctx, shape),
                     jnp.broadcast_to(a_lat, shape))


def _is_ctx_rows(t, tm, lc):
    rows = t * tm + lax.broadcasted_iota(jnp.int32, (tm, 1), 0)
    return rows < lc


def _norm_mod(x, g, mb, mc, is_ctx, k):
    y = _rms(x) * g
    shift = _row_select(is_ctx, mc[k:k + 1], mb[k:k + 1], x.shape)
    scale = _row_select(is_ctx, mc[k + 1:k + 2], mb[k + 1:k + 2], x.shape)
    return y * (1.0 + scale) + shift


def _tri_masks(c, reverse):
    r = lax.broadcasted_iota(jnp.int32, (c, c), 0)
    q = lax.broadcasted_iota(jnp.int32, (c, c), 1)
    if reverse:
        return r <= q, r < q, r == q
    return r >= q, r > q, r == q


def _as01(mask):
    return jnp.where(mask, 1.0, 0.0).astype(BF16)


def _adaln_kernel(c_ref, w_ref, b_ref, o_ref):
    s = _silu(c_ref[...])
    o_ref[0] = _dot3(s, w_ref[0]) + b_ref[0]


def _adaln(cvec, ada_w, ada_b):
    depth, d, n = ada_w.shape
    tn = _tile(n, 1536, LANES)
    return pl.pallas_call(
        _adaln_kernel,
        grid=(depth, n // tn),
        in_specs=[pl.BlockSpec((SUBLANES, d), lambda i, j: (0, 0)),
                  pl.BlockSpec((1, d, tn), lambda i, j: (i, 0, j)),
                  pl.BlockSpec((1, 1, tn), lambda i, j: (i, 0, j))],
        out_specs=pl.BlockSpec((1, SUBLANES, tn), lambda i, j: (i, 0, j)),
        out_shape=jax.ShapeDtypeStruct((depth, SUBLANES, n), F32),
        compiler_params=_params("parallel", "parallel"),
        name="adaln",
    )(cvec, ada_w, ada_b.reshape(depth, 1, n))


def _in_proj_kernel(x_ref, g_ref, mb_ref, mc_ref, w_ref, o_ref, h_scr, *, tm, lc):
    @pl.when(pl.program_id(2) == 0)
    def _():
        is_ctx = _is_ctx_rows(pl.program_id(1), tm, lc)
        h = _norm_mod(x_ref[0], g_ref[...], mb_ref[0], mc_ref[0], is_ctx, 0)
        h_scr[...] = h.astype(BF16)

    o_ref[0] = _dot(h_scr[...], w_ref[...])


def _in_proj(xa, modl, g, w, lc, tn):
    b, lt, d = xa.shape
    n = w.shape[1]
    tm = _tile(lt, 768, LANES)
    return pl.pallas_call(
        functools.partial(_in_proj_kernel, tm=tm, lc=lc),
        grid=(b, lt // tm, n // tn),
        in_specs=[pl.BlockSpec((1, tm, d), lambda i, t, j: (i, t, 0)),
                  pl.BlockSpec((1, d), lambda i, t, j: (0, 0)),
                  pl.BlockSpec((1, 6, d), lambda i, t, j: (i, 0, 0)),
                  pl.BlockSpec((1, 6, d), lambda i, t, j: (b, 0, 0)),
                  pl.BlockSpec((d, tn), lambda i, t, j: (0, j))],
        out_specs=pl.BlockSpec((1, tm, tn), lambda i, t, j: (i, t, j)),
        out_shape=jax.ShapeDtypeStruct((b, lt, n), F32),
        scratch_shapes=[pltpu.VMEM((tm, d), BF16)],
        compiler_params=_params("parallel", "parallel", "arbitrary"),
        name="in_proj",
    )(xa, g.reshape(1, d), modl, modl, w)


def _out_proj_kernel(x_ref, of_ref, ob_ref, gate_ref, ng_ref, w_ref, mb_ref, mc_ref, o_ref,
                     *, heads, dv, center, tm, lc):
    o = of_ref[0] + ob_ref[0]
    parts = []
    for h in range(heads):
        oh = o[:, h * dv:(h + 1) * dv]
        if center:
            oh = oh - jnp.mean(oh, axis=-1, keepdims=True)
        parts.append(_rms(oh))
    on = jnp.concatenate(parts, axis=-1) * ng_ref[...]
    y = (on * _silu(gate_ref[0])).astype(BF16)
    yo = _dot(y, w_ref[...])
    is_ctx = _is_ctx_rows(pl.program_id(1), tm, lc)
    gt = _row_select(is_ctx, mc_ref[0][2:3], mb_ref[0][2:3], yo.shape)
    o_ref[0] = x_ref[0] + gt * yo


def _out_proj(xa, of, ob, p, gate_blk, ng, w, modl, lc, heads, dv, center):
    b, lt, d = xa.shape
    v = heads * dv
    tm = _tile(lt, 256, LANES)
    return pl.pallas_call(
        functools.partial(_out_proj_kernel, heads=heads, dv=dv, center=center, tm=tm, lc=lc),
        grid=(b, lt // tm),
        in_specs=[pl.BlockSpec((1, tm, d), lambda i, t: (i, t, 0)),
                  pl.BlockSpec((1, tm, v), lambda i, t: (i, t, 0)),
                  pl.BlockSpec((1, tm, v), lambda i, t: (i, t, 0)),
                  pl.BlockSpec((1, tm, v), lambda i, t: (i, t, gate_blk)),
                  pl.BlockSpec((1, v), lambda i, t: (0, 0)),
                  pl.BlockSpec((v, d), lambda i, t: (0, 0)),
                  pl.BlockSpec((1, 6, d), lambda i, t: (i, 0, 0)),
                  pl.BlockSpec((1, 6, d), lambda i, t: (b, 0, 0))],
        out_specs=pl.BlockSpec((1, tm, d), lambda i, t: (i, t, 0)),
        out_shape=jax.ShapeDtypeStruct((b, lt, d), F32),
        compiler_params=_params("parallel", "parallel"),
        name="out_proj",
    )(xa, of, ob, p, ng.reshape(1, v), w, modl, modl)


def _plain_out_kernel(x_ref, z_ref, w_ref, mb_ref, o_ref):
    yo = _dot(z_ref[0].astype(BF16), w_ref[...])
    o_ref[0] = x_ref[0] + mb_ref[0][2:3] * yo


def _plain_out(xa, z, w, modl, lc):
    b, lt, d = xa.shape
    l = z.shape[1]
    tm = _tile(math.gcd(l, lc), 256, LANES)
    off = lc // tm
    return pl.pallas_call(
        _plain_out_kernel,
        grid=(b, l // tm),
        in_specs=[pl.BlockSpec((1, tm, d), lambda i, t: (i, t + off, 0)),
                  pl.BlockSpec((1, tm, d), lambda i, t: (i, t, 0)),
                  pl.BlockSpec((d, d), lambda i, t: (0, 0)),
                  pl.BlockSpec((1, 6, d), lambda i, t: (i, 0, 0))],
        out_specs=pl.BlockSpec((1, tm, d), lambda i, t: (i, t + off, 0)),
        out_shape=jax.ShapeDtypeStruct((b, lt, d), F32),
        input_output_aliases={0: 0},
        compiler_params=_params("parallel", "parallel"),
        name="hyena_out",
    )(xa, z, w, modl)


def _ffn_kernel(x_ref, g_ref, mb_ref, mc_ref, w1_ref, w3_ref, w2_ref, o_ref, h_scr, acc_scr,
                *, tm, lc, nj):
    j = pl.program_id(2)
    is_ctx = _is_ctx_rows(pl.program_id(1), tm, lc)

    @pl.when(j == 0)
    def _():
        h = _norm_mod(x_ref[0], g_ref[...], mb_ref[0], mc_ref[0], is_ctx, 3)
        h_scr[...] = h.astype(BF16)

    h = h_scr[...]
    t = (_silu(_dot(h, w1_ref[...])) * _dot(h, w3_ref[...])).astype(BF16)
    part = _dot(t, w2_ref[...])

    @pl.when(j == 0)
    def _():
        acc_scr[...] = part

    @pl.when(j > 0)
    def _():
        acc_scr[...] += part

    @pl.when(j == nj - 1)
    def _():
        gt = _row_select(is_ctx, mc_ref[0][5:6], mb_ref[0][5:6], acc_scr.shape)
        o_ref[0] = x_ref[0] + gt * acc_scr[...]


def _ffn(xa, modl, g, w1, w3, w2, lc):
    b, lt, d = xa.shape
    ff = w1.shape[1]
    tm = _tile(lt, 768, LANES)
    tf = _tile(ff, 256, LANES)
    nj = ff // tf
    return pl.pallas_call(
        functools.partial(_ffn_kernel, tm=tm, lc=lc, nj=nj),
        grid=(b, lt // tm, nj),
        in_specs=[pl.BlockSpec((1, tm, d), lambda i, t, j: (i, t, 0)),
                  pl.BlockSpec((1, d), lambda i, t, j: (0, 0)),
                  pl.BlockSpec((1, 6, d), lambda i, t, j: (i, 0, 0)),
                  pl.BlockSpec((1, 6, d), lambda i, t, j: (b, 0, 0)),
                  pl.BlockSpec((d, tf), lambda i, t, j: (0, j)),
                  pl.BlockSpec((d, tf), lambda i, t, j: (0, j)),
                  pl.BlockSpec((tf, d), lambda i, t, j: (j, 0))],
        out_specs=pl.BlockSpec((1, tm, d), lambda i, t, j: (i, t, 0)),
        out_shape=jax.ShapeDtypeStruct((b, lt, d), F32),
        scratch_shapes=[pltpu.VMEM((tm, d), BF16), pltpu.VMEM((tm, d), F32)],
        compiler_params=_params("parallel", "parallel", "arbitrary"),
        name="ffn",
    )(xa, g.reshape(1, d), modl, modl, w1, w3, w2)


def _final_norm_kernel(x_ref, g_ref, o_ref):
    o_ref[0] = _rms(x_ref[0]) * g_ref[...]


def _final_norm(xa, g, lc):
    b, lt, d = xa.shape
    l = lt - lc
    tm = _tile(math.gcd(l, lc), 512, LANES)
    off = lc // tm
    return pl.pallas_call(
        _final_norm_kernel,
        grid=(b, l // tm),
        in_specs=[pl.BlockSpec((1, tm, d), lambda i, t: (i, t + off, 0)),
                  pl.BlockSpec((1, d), lambda i, t: (0, 0))],
        out_specs=pl.BlockSpec((1, tm, d), lambda i, t: (i, t, 0)),
        out_shape=jax.ShapeDtypeStruct((b, l, d), F32),
        compiler_params=_params("parallel", "parallel"),
        name="final_norm",
    )(xa, g.reshape(1, d))


def _conv_rows(prev8, cur, next8, w, taps, row0, lc, lt):
    tr = cur.shape[0]
    win = jnp.concatenate([prev8, cur, next8], axis=0)
    rows = row0 + lax.broadcasted_iota(jnp.int32, (tr, 1), 0)
    is_ctx = rows < lc
    lo = jnp.where(is_ctx, 0, lc)
    hi = jnp.where(is_ctx, lc, lt)
    acc = None
    for j in range(taps):
        s = j - taps // 2
        sl = win[SUBLANES + s:SUBLANES + s + tr]
        if s != 0:
            src = rows + s
            ok = jnp.logical_and(src >= lo, src < hi)
            sl = jnp.where(jnp.broadcast_to(ok, sl.shape), sl, 0.0)
        term = sl * w[j:j + 1]
        acc = term if acc is None else acc + term
    return acc


def _halo_specs(tr, width, col_fn, row_off, n_rows):
    r8 = tr // SUBLANES
    last8 = n_rows // SUBLANES - 1

    def prev_map(i, t, c):
        return (i, jnp.maximum((t + row_off) * r8 - 1, 0), col_fn(c))

    def cur_map(i, t, c):
        return (i, t + row_off, col_fn(c))

    def next_map(i, t, c):
        return (i, jnp.minimum((t + row_off + 1) * r8, last8), col_fn(c))

    return [pl.BlockSpec((1, SUBLANES, width), prev_map),
            pl.BlockSpec((1, tr, width), cur_map),
            pl.BlockSpec((1, SUBLANES, width), next_map)]


def _gdn_prep_kernel(pp_ref, pc_ref, pn_ref, w_ref, o_ref, *, tr, lc, lt, taps, dk):
    part = pl.program_id(2)
    row0 = pl.program_id(1) * tr
    u = _silu(_conv_rows(pp_ref[0], pc_ref[0], pn_ref[0], w_ref[...], taps, row0, lc, lt))
    heads = u.shape[1] // dk
    normed = []
    for h in range(heads):
        uh = u[:, h * dk:(h + 1) * dk]
        normed.append(uh * lax.rsqrt(jnp.sum(uh * uh, axis=-1, keepdims=True) + EPS))
    un = jnp.concatenate(normed, axis=-1)
    scale = jnp.where(part == 0, dk ** -0.5, 1.0)
    o_ref[0, 0] = jnp.where(part == 2, u, un * scale)


def _gdn_prep(p, conv_w, lc, qk, dk):
    b, lt, _ = p.shape
    taps = conv_w.shape[0]
    tr = _tile(lt, 256, LANES)
    specs = _halo_specs(tr, qk, lambda c: c, 0, lt)
    return pl.pallas_call(
        functools.partial(_gdn_prep_kernel, tr=tr, lc=lc, lt=lt, taps=taps, dk=dk),
        grid=(b, lt // tr, 3),
        in_specs=specs + [pl.BlockSpec((taps, qk), lambda i, t, c: (0, c))],
        out_specs=pl.BlockSpec((1, 1, tr, qk), lambda i, t, c: (c, i, t, 0)),
        out_shape=jax.ShapeDtypeStruct((3, b, lt, qk), F32),
        compiler_params=_params("parallel", "parallel", "parallel"),
        name="gdn_prep",
    )(p, p, p, conv_w)


def _gdn_gates_kernel(p_ref, a_ref, dt_ref, o_ref, *, nb):
    x = p_ref[0]
    lane = lax.broadcasted_iota(jnp.int32, x.shape, 1)
    beta = jax.nn.sigmoid(x)
    g = -jnp.exp(a_ref[...]) * _softplus(x + dt_ref[...])
    o_ref[0] = jnp.where(lane < nb, beta, g)


def _gdn_gates(p, a_log, dt_bias, col_blk):
    b, lt, _ = p.shape
    nb = a_log.size
    pad = jnp.zeros((LANES,), F32)
    a_row = pad.at[nb:2 * nb].set(a_log.reshape(-1)).reshape(1, LANES)
    dt_row = pad.at[nb:2 * nb].set(dt_bias.reshape(-1)).reshape(1, LANES)
    tr = _tile(lt, 1024, LANES)
    return pl.pallas_call(
        functools.partial(_gdn_gates_kernel, nb=nb),
        grid=(b, lt // tr),
        in_specs=[pl.BlockSpec((1, tr, LANES), lambda i, t: (i, t, col_blk)),
                  pl.BlockSpec((1, LANES), lambda i, t: (0, 0)),
                  pl.BlockSpec((1, LANES), lambda i, t: (0, 0))],
        out_specs=pl.BlockSpec((1, tr, LANES), lambda i, t: (i, t, 0)),
        out_shape=jax.ShapeDtypeStruct((b, lt, LANES), F32),
        compiler_params=_params("parallel", "parallel"),
        name="gdn_gates",
    )(p, a_row, dt_row)


def _block_masks(r, reverse):
    i = lax.broadcasted_iota(jnp.int32, (r, r), 0)
    j = lax.broadcasted_iota(jnp.int32, (r, r), 1)
    same = (i // CHUNK) == (j // CHUNK)
    if reverse:
        return jnp.logical_and(same, i <= j), jnp.logical_and(same, i < j), same
    return jnp.logical_and(same, i >= j), jnp.logical_and(same, i > j), same


def _gdn_local(insts, incl, strict):
    r = insts[0][0].shape[0]
    dv = insts[0][2].shape[1]
    kbs, sols, ms, attns = [], [], [], []
    for q, k, v, beta, gc, gr, tot in insts:
        diff = jnp.broadcast_to(gc, (r, r)) - jnp.broadcast_to(gr, (r, r))
        decay = jnp.where(incl, jnp.exp(jnp.where(incl, diff, 0.0)), 0.0)
        kb = k * beta
        gram = _dot_nt(jnp.concatenate([kb, q], axis=0).astype(BF16), k.astype(BF16))
        ms.append(jnp.where(strict, gram[:r] * decay, 0.0).astype(BF16))
        attns.append((gram[r:] * decay).astype(BF16))
        sols.append(jnp.concatenate([v * beta, kb * jnp.exp(gc)], axis=-1))
    pows = [ms]
    span = 2
    while span < CHUNK:
        pows.append([_dot(p, p).astype(BF16) for p in pows[-1]])
        span *= 2
    for level in reversed(pows[1:]):
        sols = [s + _dot(p, s.astype(BF16)) for p, s in zip(level, sols)]
    sols = [s - _dot(m, s.astype(BF16)) for m, s in zip(ms, sols)]
    aws = [_dot(a, s.astype(BF16)) for a, s in zip(attns, sols)]
    outs = []
    for (q, k, v, beta, gc, gr, tot), sol, aw in zip(insts, sols, aws):
        qe = q * jnp.exp(gc) - aw[:, dv:]
        kd = k * jnp.exp(tot - gc)
        outs.append((sol[:, :dv], sol[:, dv:], qe, kd, aw[:, :dv]))
    return outs


def _gdn_intra_kernel(q_ref, k_ref, v_ref, bg_ref, wq_ref, kd_ref, u_ref, oi_ref, dg_ref, *, nheads, dk, hb):
    r = q_ref.shape[2]
    groups = r // CHUNK
    bg = bg_ref[0]
    for rev in range(2):
        incl, strict, same = _block_masks(r, bool(rev))
        gc_all = _dot01(_as01(incl), bg)
        tot_all = _dot01(_as01(same), bg)
        gr_all = gc_all.T
        for h0 in range(0, nheads, hb):
            insts = []
            for h in range(h0, h0 + hb):
                cols = slice(h * dk, (h + 1) * dk)
                ib, ig = rev * nheads + h, (2 + rev) * nheads + h
                insts.append((q_ref[0, 0, :, cols], k_ref[0, 0, :, cols], v_ref[0, 0, :, cols],
                              bg[:, ib:ib + 1], gc_all[:, ig:ig + 1], gr_all[ig:ig + 1, :],
                              tot_all[:, ig:ig + 1]))
            for h, (u, w, qe, kd, ol) in zip(range(h0, h0 + hb), _gdn_local(insts, incl, strict)):
                cols = slice(h * dk, (h + 1) * dk)
                for g in range(groups):
                    rows = slice(g * CHUNK, (g + 1) * CHUNK)
                    wq_ref[rev, 0, g, 0:CHUNK, cols] = w[rows].astype(BF16)
                    wq_ref[rev, 0, g, CHUNK:2 * CHUNK, cols] = qe[rows].astype(BF16)
                kd_ref[rev, 0, :, cols] = kd.astype(BF16)
                u_ref[rev, 0, :, cols] = u
                if rev == 0:
                    oi_ref[0, :, cols] = ol
                else:
                    oi_ref[0, :, cols] += ol
                dfull = jnp.broadcast_to(jnp.exp(tot_all[:, (2 + rev) * nheads + h:(2 + rev) * nheads + h + 1]),
                                         (r, dk))
                for g in range(groups):
                    dg_ref[rev, 0, g * SUBLANES:(g + 1) * SUBLANES, cols] = dfull[g * CHUNK:g * CHUNK + SUBLANES]


def _gdn_intra(qkv, bg, nheads, dk):
    _, b, lt, width = qkv.shape
    r = _tile(lt, 256, CHUNK)
    groups = r // CHUNK
    nc = lt // CHUNK
    part = lambda c: pl.BlockSpec((1, 1, r, width), lambda i, t: (c, i, t, 0))
    return pl.pallas_call(
        functools.partial(_gdn_intra_kernel, nheads=nheads, dk=dk, hb=4),
        grid=(b, lt // r),
        in_specs=[part(0), part(1), part(2), pl.BlockSpec((1, r, LANES), lambda i, t: (i, t, 0))],
        out_specs=[pl.BlockSpec((2, 1, groups, 2 * CHUNK, width), lambda i, t: (0, i, t, 0, 0)),
                   pl.BlockSpec((2, 1, r, width), lambda i, t: (0, i, t, 0)),
                   pl.BlockSpec((2, 1, r, width), lambda i, t: (0, i, t, 0)),
                   pl.BlockSpec((1, r, width), lambda i, t: (i, t, 0)),
                   pl.BlockSpec((2, 1, groups * SUBLANES, width), lambda i, t: (0, i, t, 0))],
        out_shape=[jax.ShapeDtypeStruct((2, b, nc, 2 * CHUNK, width), BF16),
                   jax.ShapeDtypeStruct((2, b, lt, width), BF16),
                   jax.ShapeDtypeStruct((2, b, lt, width), F32),
                   jax.ShapeDtypeStruct((b, lt, width), F32),
                   jax.ShapeDtypeStruct((2, b, nc * SUBLANES, width), F32)],
        compiler_params=_params("parallel", "parallel"),
        name="gdn_intra",
    )(qkv, qkv, qkv, bg)


def _gdn_scan_kernel(wqf_ref, kdf_ref, uf_ref, dgf_ref, wqb_ref, kdb_ref, ub_ref, dgb_ref, oi_ref,
                     of_ref, ob_ref, s_scr, *, nheads, dk):
    @pl.when(pl.program_id(1) == 0)
    def _():
        s_scr[...] = jnp.zeros_like(s_scr)

    refs = ((wqf_ref, kdf_ref, uf_ref, dgf_ref, of_ref), (wqb_ref, kdb_ref, ub_ref, dgb_ref, ob_ref))
    insts = [(rev, h, slice(h * dk, (h + 1) * dk)) for rev in range(2) for h in range(nheads)]
    states = [s_scr[rev, h] for rev, h, _ in insts]
    ws = [_dot(refs[rev][0][0, 0, 0, :, cols], s.astype(BF16))
          for (rev, h, cols), s in zip(insts, states)]
    v_new = [(refs[rev][2][0, 0, :, cols] - w[:CHUNK]).astype(BF16) for (rev, h, cols), w in zip(insts, ws)]
    upd = [_dot_tn(refs[rev][1][0, 0, :, cols], v) for (rev, h, cols), v in zip(insts, v_new)]
    for (rev, h, cols), s, w, du in zip(insts, states, ws, upd):
        o = w[CHUNK:]
        if rev == 0:
            o = o + oi_ref[0, :, cols]
        refs[rev][4][0, :, cols] = o
        s_scr[rev, h] = s * refs[rev][3][0, 0, 0:1, cols] + du


def _chunk_maps(ncc, nc):
    def fwd(n):
        return n

    def bwd(n):
        return jnp.where(n < ncc, ncc - 1 - n, nc - 1 - n + ncc)

    return fwd, bwd


def _gdn_scan(wq, kd, u, dg, oi, lc, nheads, dk):
    _, b, lt, width = kd.shape
    nc, ncc = lt // CHUNK, lc // CHUNK
    fwd, bwd = _chunk_maps(ncc, nc)
    in_specs, args = [], []
    for rev, cm in enumerate((fwd, bwd)):
        in_specs += [
            pl.BlockSpec((1, 1, 1, 2 * CHUNK, width), lambda i, n, cm=cm, rev=rev: (rev, i, cm(n), 0, 0)),
            pl.BlockSpec((1, 1, CHUNK, width), lambda i, n, cm=cm, rev=rev: (rev, i, cm(n), 0)),
            pl.BlockSpec((1, 1, CHUNK, width), lambda i, n, cm=cm, rev=rev: (rev, i, cm(n), 0)),
            pl.BlockSpec((1, 1, SUBLANES, width), lambda i, n, cm=cm, rev=rev: (rev, i, cm(n), 0))]
        args += [wq, kd, u, dg]
    in_specs.append(pl.BlockSpec((1, CHUNK, width), lambda i, n: (i, fwd(n), 0)))
    args.append(oi)
    out_specs = [pl.BlockSpec((1, CHUNK, width), lambda i, n, cm=cm: (i, cm(n), 0)) for cm in (fwd, bwd)]
    shape = jax.ShapeDtypeStruct((b, lt, width), F32)
    return pl.pallas_call(
        functools.partial(_gdn_scan_kernel, nheads=nheads, dk=dk),
        grid=(b, nc),
        in_specs=in_specs,
        out_specs=out_specs,
        out_shape=[shape, shape],
        scratch_shapes=[pltpu.VMEM((2, nheads, dk, dk), F32)],
        compiler_params=_params("parallel", "arbitrary"),
        name="gdn_scan",
    )(*args)


def _rotate(x, cos, sin):
    half = x.shape[1] // 2
    x1, x2 = x[:, :half], x[:, half:]
    return jnp.concatenate([x1 * cos - x2 * sin, x1 * sin + x2 * cos], axis=-1)


def _lin_stages(insts):
    o1 = [_dot(sc, v) for sc, qn, kn, v, s, dn in insts]
    o2 = [_dot(qn, s.astype(BF16)) for sc, qn, kn, v, s, dn in insts]
    up = [_dot_tn(kn, v) for sc, qn, kn, v, s, dn in insts]
    return [(a + b, s * dn + u) for a, b, u, (sc, qn, kn, v, s, dn) in zip(o1, o2, up, insts)]


def _ret_scan_kernel(qf_ref, kf_ref, vf_ref, cf_ref, sf_ref, qb_ref, kb_ref, vb_ref, cb_ref, sb_ref,
                     ld_ref, of_ref, ob_ref, s_scr, *, dk, dv, hb):
    @pl.when(pl.program_id(2) == 0)
    def _():
        s_scr[...] = jnp.zeros_like(s_scr)

    c = qf_ref.shape[1]
    i = lax.broadcasted_iota(jnp.int32, (c, c), 0)
    j = lax.broadcasted_iota(jnp.int32, (c, c), 1)
    pos = lax.broadcasted_iota(jnp.int32, (c, 1), 0)
    refs = ((qf_ref, kf_ref, vf_ref, cf_ref, sf_ref, of_ref), (qb_ref, kb_ref, vb_ref, cb_ref, sb_ref, ob_ref))
    keys, pre = [], []
    for rev in range(2):
        q_ref, k_ref, v_ref, c_ref, sn_ref, _ = refs[rev]
        cos, sin = c_ref[...], sn_ref[...]
        incl = (i <= j) if rev else (i >= j)
        dist = jnp.abs(i - j).astype(F32)
        steps = ((c - pos) if rev else (pos + 1)).astype(F32)
        for h in range(hb):
            lg = ld_ref[rev, h]
            q = _rotate(q_ref[0, :, h * dk:(h + 1) * dk] * dk ** -0.5, cos, sin)
            k = _rotate(k_ref[0, :, h * dk:(h + 1) * dk], cos, sin)
            decay = jnp.where(incl, jnp.exp(dist * lg[:, :1]), 0.0)
            qn = (q * jnp.exp(steps * lg)).astype(BF16)
            kn = (k * jnp.exp((c - steps) * lg)).astype(BF16)
            keys.append((rev, h))
            pre.append((q.astype(BF16), k.astype(BF16), decay, qn, kn, v_ref[0, :, h * dv:(h + 1) * dv].astype(BF16),
                        s_scr[rev, h], jnp.exp(c * lg[:, :1])))
    raw = [_dot_nt(q, k) for q, k, *_ in pre]
    insts = [((r * decay).astype(BF16), qn, kn, v, s, dn) for r, (q, k, decay, qn, kn, v, s, dn) in zip(raw, pre)]
    for (rev, h), (o, s_new) in zip(keys, _lin_stages(insts)):
        refs[rev][5][0, :, h * dv:(h + 1) * dv] = o
        s_scr[rev, h] = s_new


def _ret_scan(p, cos, sin, ld, lc, heads, dk, dv, hb):
    b, lt, _ = p.shape
    c = _tile(math.gcd(lt - lc, lc), 256, CHUNK)
    nc, ncc = lt // c, lc // c
    fwd, bwd = _chunk_maps(ncc, nc)
    half = dk // 2
    k_off = heads // hb
    v_off = 2 * heads * dk // (hb * dv)
    in_specs, args = [], []
    for cm in (fwd, bwd):
        in_specs += [pl.BlockSpec((1, c, hb * dk), lambda i, h, n, cm=cm: (i, cm(n), h)),
                     pl.BlockSpec((1, c, hb * dk), lambda i, h, n, cm=cm: (i, cm(n), k_off + h)),
                     pl.BlockSpec((1, c, hb * dv), lambda i, h, n, cm=cm: (i, cm(n), v_off + h)),
                     pl.BlockSpec((c, half), lambda i, h, n, cm=cm: (cm(n), 0)),
                     pl.BlockSpec((c, half), lambda i, h, n, cm=cm: (cm(n), 0))]
        args += [p, p, p, cos, sin]
    in_specs.append(pl.BlockSpec((2, hb, 1, dk), lambda i, h, n: (0, h, 0, 0)))
    args.append(ld)
    out_specs = [pl.BlockSpec((1, c, hb * dv), lambda i, h, n, cm=cm: (i, cm(n), h)) for cm in (fwd, bwd)]
    shape = jax.ShapeDtypeStruct((b, lt, heads * dv), F32)
    return pl.pallas_call(
        functools.partial(_ret_scan_kernel, dk=dk, dv=dv, hb=hb),
        grid=(b, heads // hb, nc),
        in_specs=in_specs,
        out_specs=out_specs,
        out_shape=[shape, shape],
        scratch_shapes=[pltpu.VMEM((2, hb, dk, dv), F32)],
        compiler_params=_params("parallel", "parallel", "arbitrary"),
        name="ret_scan",
    )(*args)


def _gla_scan_kernel(qf_ref, kf_ref, vf_ref, lf_ref, qb_ref, kb_ref, vb_ref, lb_ref,
                     gw_ref, gb_ref, of_ref, ob_ref, s_scr, *, heads, dk, dv):
    @pl.when(pl.program_id(1) == 0)
    def _():
        s_scr[...] = jnp.zeros_like(s_scr)

    c = CHUNK
    refs = ((qf_ref, kf_ref, vf_ref, lf_ref, of_ref), (qb_ref, kb_ref, vb_ref, lb_ref, ob_ref))
    ones = jnp.ones((c, LANES), BF16)
    keys, pre = [], []
    for rev in range(2):
        q_ref, k_ref, v_ref, low_ref, _ = refs[rev]
        incl, _, _ = _tri_masks(c, bool(rev))
        logit = _dot(low_ref[0].astype(BF16), gw_ref[rev]) + gb_ref[rev]
        ld = -_softplus(-logit) * (1.0 / GLA_GATE_NORM)
        cum = _dot01(_as01(incl), ld)
        tot = _dot01_tn(ld, ones)
        mid = c // 2 - 1 if rev else c // 2
        ref = cum[mid:mid + 1]
        c_last = cum[0:1] if rev else cum[c - 1:c]
        q = q_ref[0] * dk ** -0.5
        k = k_ref[0]
        qa = (q * jnp.exp(cum - ref)).astype(BF16)
        ka = (k * jnp.exp(ref - cum)).astype(BF16)
        qn = (q * jnp.exp(cum)).astype(BF16)
        kn = (k * jnp.exp(c_last - cum)).astype(BF16)
        for h in range(heads):
            cols = slice(h * dk, (h + 1) * dk)
            dn = jnp.broadcast_to(jnp.exp(tot[h * dk:(h + 1) * dk, :1]), (dk, dv))
            keys.append((rev, h))
            pre.append((qa[:, cols], ka[:, cols], incl, qn[:, cols], kn[:, cols],
                        v_ref[0, :, h * dv:(h + 1) * dv].astype(BF16), s_scr[rev, h], dn))
    raw = [_dot_nt(qa, ka) for qa, ka, *_ in pre]
    insts = [(jnp.where(incl, r, 0.0).astype(BF16), qn, kn, v, s, dn)
             for r, (qa, ka, incl, qn, kn, v, s, dn) in zip(raw, pre)]
    for (rev, h), (o, s_new) in zip(keys, _lin_stages(insts)):
        refs[rev][4][0, :, h * dv:(h + 1) * dv] = o
        s_scr[rev, h] = s_new


def _gla_scan(p, gw, gb, lc, heads, dk, dv, low_blk):
    b, lt, _ = p.shape
    nc, ncc = lt // CHUNK, lc // CHUNK
    fwd, bwd = _chunk_maps(ncc, nc)
    qk, vw = heads * dk, heads * dv
    in_specs, args = [], []
    for cm in (fwd, bwd):
        in_specs += [pl.BlockSpec((1, CHUNK, qk), lambda i, n, cm=cm: (i, cm(n), 0)),
                     pl.BlockSpec((1, CHUNK, qk), lambda i, n, cm=cm: (i, cm(n), 1)),
                     pl.BlockSpec((1, CHUNK, vw), lambda i, n, cm=cm: (i, cm(n), 2 * qk // vw)),
                     pl.BlockSpec((1, CHUNK, LANES), lambda i, n, cm=cm: (i, cm(n), low_blk))]
        args += [p, p, p, p]
    in_specs += [pl.BlockSpec((2, LANES, qk), lambda i, n: (0, 0, 0)),
                 pl.BlockSpec((2, 1, qk), lambda i, n: (0, 0, 0))]
    args += [gw, gb]
    out_specs = [pl.BlockSpec((1, CHUNK, vw), lambda i, n, cm=cm: (i, cm(n), 0)) for cm in (fwd, bwd)]
    shape = jax.ShapeDtypeStruct((b, lt, vw), F32)
    return pl.pallas_call(
        functools.partial(_gla_scan_kernel, heads=heads, dk=dk, dv=dv),
        grid=(b, nc),
        in_specs=in_specs,
        out_specs=out_specs,
        out_shape=[shape, shape],
        scratch_shapes=[pltpu.VMEM((2, heads, dk, dv), F32)],
        compiler_params=_params("parallel", "arbitrary"),
        name="gla_scan",
    )(*args)


def _axial_tables(l, lc, dim):
    rows = l // GRID_W
    row = jnp.repeat(jnp.arange(rows, dtype=F32), GRID_W)
    col = jnp.tile(jnp.arange(GRID_W, dtype=F32), rows)
    nf = dim // 4
    inv = ROPE_BASE ** (-jnp.arange(nf, dtype=F32) / nf)
    ang = jnp.concatenate([row[:, None] * inv, col[:, None] * inv], axis=-1)
    ang = jnp.concatenate([jnp.zeros((lc, dim // 2), F32), ang], axis=0)
    return jnp.cos(ang), jnp.sin(ang)


def _hy_conv_kernel(pp_ref, pc_ref, pn_ref, w_ref, o_ref, *, tr, lc, lt, taps, row_off):
    row0 = (pl.program_id(1) + row_off) * tr
    o_ref[0, 0] = _conv_rows(pp_ref[0], pc_ref[0], pn_ref[0], w_ref[...], taps, row0, lc, lt)


def _hy_conv(p, conv_w, lc, d):
    b, lt, _ = p.shape
    l = lt - lc
    taps = conv_w.shape[0]
    tr = _tile(math.gcd(l, lc), 256, LANES)
    row_off = lc // tr
    specs = _halo_specs(tr, d, lambda c: c, row_off, lt)
    return pl.pallas_call(
        functools.partial(_hy_conv_kernel, tr=tr, lc=lc, lt=lt, taps=taps, row_off=row_off),
        grid=(b, l // tr, 3),
        in_specs=specs + [pl.BlockSpec((taps, d), lambda i, t, c: (0, c))],
        out_specs=pl.BlockSpec((1, 1, tr, d), lambda i, t, c: (c, i, t, 0)),
        out_shape=jax.ShapeDtypeStruct((3, b, l, d), F32),
        compiler_params=_params("parallel", "parallel", "parallel"),
        name="hyena_conv",
    )(p, p, p, conv_w)


def _hy_filter_kernel(z_ref, w1_ref, b1_ref, w2_ref, b2_ref, w3_ref, fr_ref, dl_ref, f_ref, s_ref, *, tr, l):
    t = pl.program_id(0)
    fr = fr_ref[...]
    hid = jnp.sin(fr * (_dot3(z_ref[...], w1_ref[...]) + b1_ref[...]))
    hid = jnp.sin(fr * (_dot3(hid, w2_ref[...]) + b2_ref[...]))
    filt = _dot3(hid, w3_ref[...])
    pos = (t * tr + lax.broadcasted_iota(jnp.int32, (tr, 1), 0)).astype(F32)
    dist = jnp.abs(pos - (l // 2)) / (l // 2)
    filt = filt * jnp.exp(-dist * dl_ref[...])
    f_ref[...] = filt
    part = jnp.sum(jnp.abs(filt), axis=0, keepdims=True)

    @pl.when(t == 0)
    def _():
        s_ref[...] = part

    @pl.when(t > 0)
    def _():
        s_ref[...] += part


def _hy_filter(l, w1, b1, w2, b2, w3, fr, d):
    emb, hf = w1.shape
    n_out = w3.shape[1]
    pos = jnp.arange(l, dtype=F32)
    tt = pos / (l - 1)
    bands = (emb - 1) // 2
    freqs = jnp.linspace(1e-4, bands - 1, bands, dtype=F32)
    ang = (2.0 * math.pi / l) * pos[:, None] * freqs[None, :]
    z = jnp.concatenate([tt[:, None], jnp.cos(ang), -jnp.sin(ang)], axis=-1)
    z = jnp.pad(z, ((0, 0), (0, LANES - emb)))
    w1p = jnp.pad(w1, ((0, LANES - emb), (0, 0)))
    deltas = jnp.abs(jnp.linspace(math.log(HY_TARGET) / HY_SLOW, math.log(HY_TARGET) / HY_FAST, d, dtype=F32))
    dl = jnp.tile(deltas, n_out // d).reshape(1, n_out)
    tr = _tile(l, 256, LANES)
    full = lambda shape: pl.BlockSpec(shape, lambda t: (0,) * len(shape))
    return pl.pallas_call(
        functools.partial(_hy_filter_kernel, tr=tr, l=l),
        grid=(l // tr,),
        in_specs=[pl.BlockSpec((tr, LANES), lambda t: (t, 0)),
                  full((LANES, hf)), full((1, hf)), full((hf, hf)), full((1, hf)), full((hf, n_out)),
                  full((1, hf)), full((1, n_out))],
        out_specs=[pl.BlockSpec((tr, n_out), lambda t: (t, 0)), full((1, n_out))],
        out_shape=[jax.ShapeDtypeStruct((l, n_out), F32), jax.ShapeDtypeStruct((1, n_out), F32)],
        compiler_params=_params("arbitrary"),
        name="hyena_filter",
    )(z, w1p, b1.reshape(1, hf), w2, b2.reshape(1, hf), w3, fr.reshape(1, hf), dl)


def _dft_consts(l):
    n = 2 * l
    n1 = math.isqrt(n)
    assert n1 * n1 == n and n1 % 4 == 0, n
    h = n1 // 2
    idx = np.arange(n1)
    ang = 2.0 * np.pi * np.outer(idx, idx[:h]) / n1
    c, s = np.cos(ang), np.sin(ang)
    fa = np.block([[c, s], [-s, c]])
    ang = 2.0 * np.pi * np.outer(idx[:h] + n1 // 4, idx) / n1
    c, s = np.cos(ang) / n, np.sin(ang) / n
    fc = np.block([[c, -s], [s, c]])
    return n1, jnp.asarray(fa, F32), jnp.asarray(fc, F32)


def _mid_tables(n1):
    n = n1 * n1
    k1 = jnp.arange(n1, dtype=jnp.int32)[:, None, None]
    a = jnp.arange(n1, dtype=jnp.int32)[None, :, None]
    bb = jnp.arange(n1, dtype=jnp.int32)[None, None, :]
    mf = (bb * (n1 * a + k1)) % n
    mi = (a * (n1 * bb + k1)) % n
    out = []
    for m, sign in ((mf, -1.0), (mi, 1.0)):
        ang = m.astype(F32) * (2.0 * math.pi / n)
        c, s = jnp.cos(ang), sign * jnp.sin(ang)
        top = jnp.concatenate([c, -s], axis=2)
        bot = jnp.concatenate([s, c], axis=2)
        out.append(jnp.concatenate([top, bot], axis=1))
    return out


def _fft_a_kernel(z_ref, fa_ref, o_ref, *, n1):
    z = z_ref[...]
    zz = z.reshape(z.shape[0] * z.shape[1], z.shape[2]).astype(BF16)
    out = _dot(fa_ref[...], zz)
    o_ref[0, 0] = out[:n1]
    o_ref[0, 1] = out[n1:]


def _fft_a(u, fa, n1, pairs):
    g, h, w = u.shape
    tl = _tile(w, 2048, LANES)
    fa_b = fa[:, :pairs * h].astype(BF16)
    return pl.pallas_call(
        functools.partial(_fft_a_kernel, n1=n1),
        grid=(g // pairs, w // tl),
        in_specs=[pl.BlockSpec((pairs, h, tl), lambda p, j: (p, 0, j)),
                  pl.BlockSpec((2 * n1, pairs * h), lambda p, j: (0, 0))],
        out_specs=pl.BlockSpec((1, 2, n1, tl), lambda p, j: (p, 0, 0, j)),
        out_shape=jax.ShapeDtypeStruct((g // pairs, 2, n1, w), F32),
        compiler_params=_params("parallel", "parallel"),
        name="fft_first",
    )(u, fa_b)


def _fft_spec_kernel(b_ref, tf_ref, s_ref, o_ref, *, n1):
    x = jnp.concatenate([b_ref[0, 0, 0], b_ref[0, 1, 0]], axis=0).astype(BF16)
    a = _dot(tf_ref[0], x) * (1.0 / s_ref[...])
    o_ref[0, 0] = a[:n1]
    o_ref[1, 0] = a[n1:]


def _fft_spec(bc, tf, norm, n1, dcols):
    bc5 = bc.reshape(1, 2, n1, n1, dcols)
    return pl.pallas_call(
        functools.partial(_fft_spec_kernel, n1=n1),
        grid=(n1,),
        in_specs=[pl.BlockSpec((1, 2, 1, n1, dcols), lambda k: (0, 0, k, 0, 0)),
                  pl.BlockSpec((1, 2 * n1, 2 * n1), lambda k: (k, 0, 0)),
                  pl.BlockSpec((1, dcols), lambda k: (0, 0))],
        out_specs=pl.BlockSpec((2, 1, n1, dcols), lambda k: (0, k, 0, 0)),
        out_shape=jax.ShapeDtypeStruct((2, n1, n1, dcols), F32),
        compiler_params=_params("parallel"),
        name="fft_filter_spectrum",
    )(bc5, tf, norm)


def _fft_mid_kernel(b_ref, tf_ref, ti_ref, f_ref, o_ref, *, n1):
    x = jnp.concatenate([b_ref[0, 0, 0], b_ref[0, 1, 0]], axis=0).astype(BF16)
    a = _dot(tf_ref[0], x)
    ar, ai = a[:n1], a[n1:]
    fr, fi = f_ref[0, 0], f_ref[1, 0]
    y = jnp.concatenate([ar * fr - ai * fi, ar * fi + ai * fr], axis=0).astype(BF16)
    gq = _dot(ti_ref[0], y)
    o_ref[0, 0, 0] = gq[:n1]
    o_ref[0, 1, 0] = gq[n1:]


def _fft_mid(bc, tf, ti, spec, order, n1, d):
    npair = bc.shape[0]
    bc5 = bc.reshape(npair, 2, n1, n1, d)
    return pl.pallas_call(
        functools.partial(_fft_mid_kernel, n1=n1),
        grid=(npair, n1),
        in_specs=[pl.BlockSpec((1, 2, 1, n1, d), lambda p, k: (p, 0, k, 0, 0)),
                  pl.BlockSpec((1, 2 * n1, 2 * n1), lambda p, k: (k, 0, 0)),
                  pl.BlockSpec((1, 2 * n1, 2 * n1), lambda p, k: (k, 0, 0)),
                  pl.BlockSpec((2, 1, n1, d), lambda p, k: (0, k, 0, order))],
        out_specs=pl.BlockSpec((1, 2, 1, n1, d), lambda p, k: (p, 0, k, 0, 0)),
        out_shape=jax.ShapeDtypeStruct((npair, 2, n1, n1, d), F32),
        compiler_params=_params("parallel", "parallel"),
        name="fft_mid",
    )(bc5, tf, ti, spec)


def _fft_c_kernel(g_ref, fc_ref, u_ref, x_ref, sk_ref, o_ref, *, h):
    x = jnp.concatenate([g_ref[0, 0], g_ref[0, 1]], axis=0).astype(BF16)
    y = _dot(fc_ref[...], x)
    reps = u_ref.shape[2] // sk_ref.shape[1]
    skip = jnp.concatenate([sk_ref[...]] * reps, axis=1)
    o_ref[0] = x_ref[0] * (y[:h] + u_ref[0] * skip)
    o_ref[1] = x_ref[1] * (y[h:] + u_ref[1] * skip)


def _fft_c(gc, fc, u, xg, skip, n1, d):
    npair = gc.shape[0]
    g, h, w = u.shape
    gc4 = gc.reshape(npair, 2, n1, w)
    tl = _tile(w, 2048, d)
    return pl.pallas_call(
        functools.partial(_fft_c_kernel, h=h),
        grid=(npair, w // tl),
        in_specs=[pl.BlockSpec((1, 2, n1, tl), lambda p, j: (p, 0, 0, j)),
                  pl.BlockSpec((2 * h, 2 * n1), lambda p, j: (0, 0)),
                  pl.BlockSpec((2, h, tl), lambda p, j: (p, 0, j)),
                  pl.BlockSpec((2, h, tl), lambda p, j: (p, 0, j)),
                  pl.BlockSpec((1, d), lambda p, j: (0, 0))],
        out_specs=pl.BlockSpec((2, h, tl), lambda p, j: (p, 0, j)),
        out_shape=jax.ShapeDtypeStruct((g, h, w), F32),
        compiler_params=_params("parallel", "parallel"),
        name="fft_last",
    )(gc4, fc.astype(BF16), u, xg, skip.reshape(1, d))


def _pad_cols(w, n):
    return jnp.pad(w, ((0, 0), (0, n - w.shape[1])))


def _gdn_layer(xa, modl, g1, lc, w_in, conv_w, a_log, dt_bias, norm_g, w_out):
    d = xa.shape[2]
    dk = 128
    heads = d // dk
    qk = heads * dk
    n_in = w_in.shape[1]
    n_pad = -(-n_in // 384) * 384
    p = _in_proj(xa, modl, g1, _pad_cols(w_in, n_pad).astype(BF16), lc, 384)
    qkv = _gdn_prep(p, conv_w, lc, qk, dk)
    bg = _gdn_gates(p, a_log, dt_bias, 4 * qk // LANES)
    wq, kd, u, oi, dg = _gdn_intra(qkv, bg, heads, dk)
    of, ob = _gdn_scan(wq, kd, u, dg, oi, lc, heads, dk)
    return _out_proj(xa, of, ob, p, 3, jnp.tile(norm_g, heads), w_out.astype(BF16), modl, lc, heads, dk, False)


def _ret_layer(xa, modl, g1, lc, w_in, norm_g, w_out):
    d = xa.shape[2]
    heads, dk, dv = d // 256, 256, 512
    p = _in_proj(xa, modl, g1, w_in.astype(BF16), lc, 768)
    cos, sin = _axial_tables(xa.shape[1] - lc, lc, dk)
    log_gamma = jnp.log1p(-jnp.exp2(-5.0 - jnp.arange(heads, dtype=F32)))
    ld = jnp.stack([log_gamma, log_gamma[::-1]])
    ld = jnp.broadcast_to(ld[:, :, None, None], (2, heads, 1, dk))
    of, ob = _ret_scan(p, cos, sin, ld, lc, heads, dk, dv, 2)
    return _out_proj(xa, of, ob, p, 2, norm_g.reshape(-1), w_out.astype(BF16), modl, lc, heads, dv, True)


def _gla_layer(xa, modl, g1, lc, w_in, gate_w2, gate_b, norm_g, w_out):
    d = xa.shape[2]
    heads = 4
    dk, dv = d // 2 // heads, d // heads
    rank = gate_w2.shape[1]
    n_in = w_in.shape[1]
    n_pad = -(-n_in // 640) * 640
    p = _in_proj(xa, modl, g1, _pad_cols(w_in, n_pad).astype(BF16), lc, 640)
    low_blk = (2 * heads * dk + 2 * heads * dv) // LANES
    gw = jnp.zeros((2, LANES, heads * dk), F32)
    for dr in range(2):
        gw = gw.at[dr, dr * rank:(dr + 1) * rank, :].set(gate_w2[dr])
    gb = gate_b.reshape(2, 1, heads * dk)
    of, ob = _gla_scan(p, gw.astype(BF16), gb, lc, heads, dk, dv, low_blk)
    return _out_proj(xa, of, ob, p, 2, jnp.tile(norm_g, heads), w_out.astype(BF16), modl, lc, heads, dv, False)


def _hyena_layer(xa, modl, g1, lc, w_in, conv_w, w1, b1, w2, b2, w3, fr, skip, w_out):
    b, lt, d = xa.shape
    l = lt - lc
    assert b % 2 == 0
    n1, fa, fc = _dft_consts(l)
    h = n1 // 2
    tf, ti = _mid_tables(n1)
    tf, ti = tf.astype(BF16), ti.astype(BF16)
    order = w3.shape[1] // d

    filt, norm = _hy_filter(l, w1, b1, w2, b2, w3, fr, d)
    fb = _fft_a(filt.reshape(1, h, n1 * order * d), fa, n1, 1)
    spec = _fft_spec(fb, tf, norm, n1, order * d)

    p = _in_proj(xa, modl, g1, w_in.astype(BF16), lc, 768)
    vx = _hy_conv(p, conv_w, lc, d).reshape(3, b, h, n1 * d)
    z = vx[0]
    for o in range(order):
        bc = _fft_a(z, fa, n1, 2)
        gq = _fft_mid(bc, tf, ti, spec, o, n1, d)
        z = _fft_c(gq, fc, z, vx[1 + o], skip[o], n1, d)
    return _plain_out(xa, z.reshape(b, l, d), w_out.astype(BF16), modl, lc)


def kernel(x, c, ctx, c_ctx, ada_w, ada_b, norm1_g, norm2_g, ffn_w1, ffn_w3, ffn_w2, gdn_w_in, gdn_conv_w, gdn_a_log, gdn_dt_bias, gdn_norm_g, gdn_w_out, ret_w_in, ret_norm_g, ret_w_out, gla_w_in, gla_gate_w2, gla_gate_b, gla_norm_g, gla_w_out, hy_w_in, hy_conv_w, hy_ff_w1, hy_ff_b1, hy_ff_w2, hy_ff_b2, hy_ff_w3, hy_sin_freq, hy_skip, hy_w_out, final_norm_g):
    b, l, d = x.shape
    lc = ctx.shape[1]
    depth = ada_w.shape[0]
    assert b < SUBLANES and lc % CHUNK == 0 and l % CHUNK == 0
    xa = jnp.concatenate([ctx, x], axis=1)
    cvec = jnp.zeros((SUBLANES, d), F32).at[:b].set(c).at[b].set(c_ctx)
    mod = _adaln(cvec, ada_w, ada_b).reshape(depth, SUBLANES, 6, d)
    for i in range(depth):
        kind, j = i % N_MIXERS, i // N_MIXERS
        modl, g1 = mod[i], norm1_g[i]
        if kind == 0:
            xa = _gdn_layer(xa, modl, g1, lc, gdn_w_in[j], gdn_conv_w[j], gdn_a_log[j], gdn_dt_bias[j],
                            gdn_norm_g[j], gdn_w_out[j])
        elif kind == 1:
            xa = _ret_layer(xa, modl, g1, lc, ret_w_in[j], ret_norm_g[j], ret_w_out[j])
        elif kind == 2:
            xa = _gla_layer(xa, modl, g1, lc, gla_w_in[j], gla_gate_w2[j], gla_gate_b[j], gla_norm_g[j],
                            gla_w_out[j])
        else:
            xa = _hyena_layer(xa, modl, g1, lc, hy_w_in[j], hy_conv_w[j], hy_ff_w1[j], hy_ff_b1[j], hy_ff_w2[j],
                              hy_ff_b2[j], hy_ff_w3[j], hy_sin_freq[j], hy_skip[j], hy_w_out[j])
        xa = _ffn(xa, modl, norm2_g[i], ffn_w1[i].astype(BF16), ffn_w3[i].astype(BF16), ffn_w2[i].astype(BF16), lc)
    return _final_norm(xa, final_norm_g, lc)
```

```python
import functools
import math

import numpy as np
import jax
import jax.numpy as jnp
from jax import lax
from jax.experimental import pallas as pl
from jax.experimental.pallas import tpu as pltpu

F32 = jnp.float32
BF16 = jnp.bfloat16
EPS = 1e-6
CHUNK = 64
N_MIXERS = 4
GRID_W = 64
ROPE_BASE = 10000.0
GLA_GATE_NORM = 16.0
HY_TARGET, HY_FAST, HY_SLOW = 1e-2, 0.3, 1.5
LANES = 128
SUBLANES = 8
VMEM_LIMIT = 56 * 1024 * 1024


def _params(*sem):
    return pltpu.CompilerParams(dimension_semantics=sem, vmem_limit_bytes=VMEM_LIMIT)


def _tile(total, target, quantum):
    assert total % quantum == 0, (total, quantum)
    best, t = quantum, quantum
    while t <= min(total, target):
        if total % t == 0:
            best = t
        t += quantum
    return best


def _dot(a, b):
    return jnp.dot(a, b, preferred_element_type=F32)


def _dot_nt(a, b):
    return lax.dot_general(a, b, (((1,), (1,)), ((), ())), preferred_element_type=F32)


def _dot_tn(a, b):
    return lax.dot_general(a, b, (((0,), (0,)), ((), ())), preferred_element_type=F32)


def _split2(x):
    hi = x.astype(BF16)
    lo = (x - hi.astype(F32)).astype(BF16)
    return hi, lo


def _split3(x):
    hi = x.astype(BF16)
    r = x - hi.astype(F32)
    mid = r.astype(BF16)
    lo = (r - mid.astype(F32)).astype(BF16)
    return hi, mid, lo


def _dot3(a, b, dot=_dot):
    ah, al = _split2(a)
    bh, bl = _split2(b)
    return dot(ah, bh) + (dot(ah, bl) + dot(al, bh))


def _dot01(m01, x, dot=_dot):
    h, m, l = _split3(x)
    return dot(m01, h) + (dot(m01, m) + dot(m01, l))


def _dot01_tn(x, m01):
    h, m, l = _split3(x)
    return _dot_tn(h, m01) + (_dot_tn(m, m01) + _dot_tn(l, m01))


def _silu(x):
    return x * jax.nn.sigmoid(x)


def _softplus(x):
    return jnp.maximum(x, 0.0) + jnp.log(1.0 + jnp.exp(-jnp.abs(x)))


def _rms(x):
    return x * lax.rsqrt(jnp.mean(x * x, axis=-1, keepdims=True) + EPS)


def _row_select(rows_ctx, a_ctx, a_lat, shape):
    return jnp.where(jnp.broadcast_to(rows_ctx, shape), jnp.broadcast_to(a_ctx, shape),
                     jnp.broadcast_to(a_lat, shape))


def _is_ctx_rows(t, tm, lc):
    rows = t * tm + lax.broadcasted_iota(jnp.int32, (tm, 1), 0)
    return rows < lc


def _norm_mod(x, g, mb, mc, is_ctx, k):
    y = _rms(x) * g
    shift = _row_select(is_ctx, mc[k:k + 1], mb[k:k + 1], x.shape)
    scale = _row_select(is_ctx, mc[k + 1:k + 2], mb[k + 1:k + 2], x.shape)
    return y * (1.0 + scale) + shift


def _tri_masks(c, reverse):
    r = lax.broadcasted_iota(jnp.int32, (c, c), 0)
    q = lax.broadcasted_iota(jnp.int32, (c, c), 1)
    if reverse:
        return r <= q, r < q, r == q
    return r >= q, r > q, r == q


def _as01(mask):
    return jnp.where(mask, 1.0, 0.0).astype(BF16)


def _adaln_kernel(c_ref, w_ref, b_ref, o_ref):
    s = _silu(c_ref[...])
    o_ref[0] = _dot3(s, w_ref[0]) + b_ref[0]


def _adaln(cvec, ada_w, ada_b):
    depth, d, n = ada_w.shape
    tn = _tile(n, 1536, LANES)
    return pl.pallas_call(
        _adaln_kernel,
        grid=(depth, n // tn),
        in_specs=[pl.BlockSpec((SUBLANES, d), lambda i, j: (0, 0)),
                  pl.BlockSpec((1, d, tn), lambda i, j: (i, 0, j)),
                  pl.BlockSpec((1, 1, tn), lambda i, j: (i, 0, j))],
        out_specs=pl.BlockSpec((1, SUBLANES, tn), lambda i, j: (i, 0, j)),
        out_shape=jax.ShapeDtypeStruct((depth, SUBLANES, n), F32),
        compiler_params=_params("parallel", "parallel"),
        name="adaln",
    )(cvec, ada_w, ada_b.reshape(depth, 1, n))


def _resident(shape):
    return pl.BlockSpec(shape, lambda *_: (0,) * len(shape), pipeline_mode=pl.Buffered(1))


def _in_proj_kernel(x_ref, g_ref, mb_ref, mc_ref, w_ref, *rest, tm, lc, tn, tail):
    if tail:
        wt_ref, o_ref, ot_ref, h_scr = rest
    else:
        o_ref, h_scr = rest
    is_ctx = _is_ctx_rows(pl.program_id(1), tm, lc)
    h_scr[...] = _norm_mod(x_ref[0], g_ref[...], mb_ref[0], mc_ref[0], is_ctx, 0).astype(BF16)
    for c in range(w_ref.shape[1] // tn):
        cols = slice(c * tn, (c + 1) * tn)
        o_ref[0, :, cols] = _dot(h_scr[...], w_ref[:, cols]).astype(o_ref.dtype)
    if tail:
        ot_ref[0] = _dot(h_scr[...], wt_ref[...])


def _in_proj(xa, modl, g, w, lc, w_tail=None):
    b, lt, d = xa.shape
    n = w.shape[1]
    tm = _tile(lt, 768, LANES)
    tn = _tile(n, 512, LANES)
    tail = w_tail is not None
    in_specs = [pl.BlockSpec((1, tm, d), lambda i, t: (i, t, 0)),
                pl.BlockSpec((1, d), lambda i, t: (0, 0)),
                pl.BlockSpec((1, 6, d), lambda i, t: (i, 0, 0)),
                pl.BlockSpec((1, 6, d), lambda i, t: (b, 0, 0)),
                _resident((d, n))]
    out_specs = [pl.BlockSpec((1, tm, n), lambda i, t: (i, t, 0))]
    out_shape = [jax.ShapeDtypeStruct((b, lt, n), BF16)]
    args = [xa, g.reshape(1, d), modl, modl, w]
    if tail:
        in_specs.append(_resident((d, LANES)))
        out_specs.append(pl.BlockSpec((1, tm, LANES), lambda i, t: (i, t, 0)))
        out_shape.append(jax.ShapeDtypeStruct((b, lt, LANES), F32))
        args.append(w_tail)
    out = pl.pallas_call(
        functools.partial(_in_proj_kernel, tm=tm, lc=lc, tn=tn, tail=tail),
        grid=(b, lt // tm),
        in_specs=in_specs,
        out_specs=out_specs,
        out_shape=out_shape,
        scratch_shapes=[pltpu.VMEM((tm, d), BF16)],
        compiler_params=_params("parallel", "parallel"),
        name="in_proj",
    )(*args)
    return (out[0], out[1]) if tail else out[0]


def _out_proj_kernel(x_ref, of_ref, ob_ref, gate_ref, ng_ref, w_ref, mb_ref, mc_ref, o_ref,
                     *, heads, dv, center, tm, lc):
    o = of_ref[0].astype(F32) + ob_ref[0].astype(F32)
    parts = []
    for h in range(heads):
        oh = o[:, h * dv:(h + 1) * dv]
        if center:
            oh = oh - jnp.mean(oh, axis=-1, keepdims=True)
        parts.append(_rms(oh))
    on = jnp.concatenate(parts, axis=-1) * ng_ref[...]
    y = (on * _silu(gate_ref[0].astype(F32))).astype(BF16)
    yo = _dot(y, w_ref[...])
    is_ctx = _is_ctx_rows(pl.program_id(1), tm, lc)
    gt = _row_select(is_ctx, mc_ref[0][2:3], mb_ref[0][2:3], yo.shape)
    o_ref[0] = x_ref[0] + gt * yo


def _out_proj(xa, of, ob, p, gate_blk, ng, w, modl, lc, heads, dv, center):
    b, lt, d = xa.shape
    v = heads * dv
    tm = _tile(lt, 256, LANES)
    return pl.pallas_call(
        functools.partial(_out_proj_kernel, heads=heads, dv=dv, center=center, tm=tm, lc=lc),
        grid=(b, lt // tm),
        in_specs=[pl.BlockSpec((1, tm, d), lambda i, t: (i, t, 0)),
                  pl.BlockSpec((1, tm, v), lambda i, t: (i, t, 0)),
                  pl.BlockSpec((1, tm, v), lambda i, t: (i, t, 0)),
                  pl.BlockSpec((1, tm, v), lambda i, t: (i, t, gate_blk)),
                  pl.BlockSpec((1, v), lambda i, t: (0, 0)),
                  pl.BlockSpec((v, d), lambda i, t: (0, 0)),
                  pl.BlockSpec((1, 6, d), lambda i, t: (i, 0, 0)),
                  pl.BlockSpec((1, 6, d), lambda i, t: (b, 0, 0))],
        out_specs=pl.BlockSpec((1, tm, d), lambda i, t: (i, t, 0)),
        out_shape=jax.ShapeDtypeStruct((b, lt, d), F32),
        compiler_params=_params("parallel", "parallel"),
        name="out_proj",
    )(xa, of, ob, p, ng.reshape(1, v), w, modl, modl)


def _plain_out_kernel(x_ref, z_ref, w_ref, mb_ref, o_ref):
    yo = _dot(z_ref[0].astype(BF16), w_ref[...])
    o_ref[0] = x_ref[0] + mb_ref[0][2:3] * yo


def _plain_out(xa, z, w, modl, lc):
    b, lt, d = xa.shape
    l = z.shape[1]
    tm = _tile(math.gcd(l, lc), 256, LANES)
    off = lc // tm
    return pl.pallas_call(
        _plain_out_kernel,
        grid=(b, l // tm),
        in_specs=[pl.BlockSpec((1, tm, d), lambda i, t: (i, t + off, 0)),
                  pl.BlockSpec((1, tm, d), lambda i, t: (i, t, 0)),
                  pl.BlockSpec((d, d), lambda i, t: (0, 0)),
                  pl.BlockSpec((1, 6, d), lambda i, t: (i, 0, 0))],
        out_specs=pl.BlockSpec((1, tm, d), lambda i, t: (i, t + off, 0)),
        out_shape=jax.ShapeDtypeStruct((b, lt, d), F32),
        input_output_aliases={0: 0},
        compiler_params=_params("parallel", "parallel"),
        name="hyena_out",
    )(xa, z, w, modl)


def _ffn_kernel(x_ref, g_ref, mb_ref, mc_ref, w1_ref, w3_ref, w2_ref, o_ref, h_scr, t_scr, *, tm, lc, tf):
    is_ctx = _is_ctx_rows(pl.program_id(1), tm, lc)
    h_scr[...] = _norm_mod(x_ref[0], g_ref[...], mb_ref[0], mc_ref[0], is_ctx, 3).astype(BF16)
    for c in range(w1_ref.shape[1] // tf):
        cols = slice(c * tf, (c + 1) * tf)
        h = h_scr[...]
        t_scr[:, cols] = (_silu(_dot(h, w1_ref[:, cols])) * _dot(h, w3_ref[:, cols])).astype(BF16)
    y = _dot(t_scr[...], w2_ref[...])
    gt = _row_select(is_ctx, mc_ref[0][5:6], mb_ref[0][5:6], y.shape)
    o_ref[0] = x_ref[0] + gt * y


def _ffn(xa, modl, g, w1, w3, w2, lc):
    b, lt, d = xa.shape
    ff = w1.shape[1]
    tm = _tile(lt, 768, LANES)
    tf = _tile(ff, 256, LANES)
    return pl.pallas_call(
        functools.partial(_ffn_kernel, tm=tm, lc=lc, tf=tf),
        grid=(b, lt // tm),
        in_specs=[pl.BlockSpec((1, tm, d), lambda i, t: (i, t, 0)),
                  pl.BlockSpec((1, d), lambda i, t: (0, 0)),
                  pl.BlockSpec((1, 6, d), lambda i, t: (i, 0, 0)),
                  pl.BlockSpec((1, 6, d), lambda i, t: (b, 0, 0)),
                  _resident((d, ff)), _resident((d, ff)), _resident((ff, d))],
        out_specs=pl.BlockSpec((1, tm, d), lambda i, t: (i, t, 0)),
        out_shape=jax.ShapeDtypeStruct((b, lt, d), F32),
        scratch_shapes=[pltpu.VMEM((tm, d), BF16), pltpu.VMEM((tm, ff), BF16)],
        compiler_params=_params("parallel", "parallel"),
        name="ffn",
    )(xa, g.reshape(1, d), modl, modl, w1, w3, w2)


def _final_norm_kernel(x_ref, g_ref, o_ref):
    o_ref[0] = _rms(x_ref[0]) * g_ref[...]


def _final_norm(xa, g, lc):
    b, lt, d = xa.shape
    l = lt - lc
    tm = _tile(math.gcd(l, lc), 512, LANES)
    off = lc // tm
    return pl.pallas_call(
        _final_norm_kernel,
        grid=(b, l // tm),
        in_specs=[pl.BlockSpec((1, tm, d), lambda i, t: (i, t + off, 0)),
                  pl.BlockSpec((1, d), lambda i, t: (0, 0))],
        out_specs=pl.BlockSpec((1, tm, d), lambda i, t: (i, t, 0)),
        out_shape=jax.ShapeDtypeStruct((b, l, d), F32),
        compiler_params=_params("parallel", "parallel"),
        name="final_norm",
    )(xa, g.reshape(1, d))


HALO = 16


def _conv_rows(prev, cur, nxt, w, taps, row0, lc, lt):
    tr = cur.shape[0]
    assert lc % tr == 0 and lt % tr == 0 and taps // 2 <= HALO
    first = jnp.logical_or(row0 == 0, row0 == lc)
    last = jnp.logical_or(row0 + tr == lc, row0 + tr == lt)
    prev = jnp.where(first, 0.0, prev.astype(F32))
    nxt = jnp.where(last, 0.0, nxt.astype(F32))
    win = jnp.concatenate([prev, cur.astype(F32), nxt], axis=0)
    acc = None
    for j in range(taps):
        s = j - taps // 2
        term = win[HALO + s:HALO + s + tr] * w[j:j + 1]
        acc = term if acc is None else acc + term
    return acc


def _halo_specs(tr, width, col_fn, row_off, n_rows):
    r8 = tr // HALO
    last8 = n_rows // HALO - 1

    def prev_map(i, t, c):
        return (i, jnp.maximum((t + row_off) * r8 - 1, 0), col_fn(c))

    def cur_map(i, t, c):
        return (i, t + row_off, col_fn(c))

    def next_map(i, t, c):
        return (i, jnp.minimum((t + row_off + 1) * r8, last8), col_fn(c))

    return [pl.BlockSpec((1, HALO, width), prev_map),
            pl.BlockSpec((1, tr, width), cur_map),
            pl.BlockSpec((1, HALO, width), next_map)]


def _gdn_prep_kernel(pp_ref, pc_ref, pn_ref, w_ref, o_ref, *, tr, lc, lt, taps, dk):
    part = pl.program_id(2)
    row0 = pl.program_id(1) * tr
    u = _silu(_conv_rows(pp_ref[0], pc_ref[0], pn_ref[0], w_ref[...], taps, row0, lc, lt))
    heads = u.shape[1] // dk
    normed = []
    for h in range(heads):
        uh = u[:, h * dk:(h + 1) * dk]
        normed.append(uh * lax.rsqrt(jnp.sum(uh * uh, axis=-1, keepdims=True) + EPS))
    un = jnp.concatenate(normed, axis=-1)
    scale = jnp.where(part == 0, dk ** -0.5, 1.0)
    o_ref[0, 0] = jnp.where(part == 2, u, un * scale)


def _gdn_prep(p, conv_w, lc, qk, dk):
    b, lt, _ = p.shape
    taps = conv_w.shape[0]
    tr = _tile(math.gcd(lt, lc), 256, LANES)
    specs = _halo_specs(tr, qk, lambda c: c, 0, lt)
    return pl.pallas_call(
        functools.partial(_gdn_prep_kernel, tr=tr, lc=lc, lt=lt, taps=taps, dk=dk),
        grid=(b, lt // tr, 3),
        in_specs=specs + [pl.BlockSpec((taps, qk), lambda i, t, c: (0, c))],
        out_specs=pl.BlockSpec((1, 1, tr, qk), lambda i, t, c: (c, i, t, 0)),
        out_shape=jax.ShapeDtypeStruct((3, b, lt, qk), F32),
        compiler_params=_params("parallel", "parallel", "parallel"),
        name="gdn_prep",
    )(p, p, p, conv_w)


def _gdn_gates_kernel(p_ref, a_ref, dt_ref, o_ref, *, nb):
    x = p_ref[0]
    lane = lax.broadcasted_iota(jnp.int32, x.shape, 1)
    beta = jax.nn.sigmoid(x)
    g = -jnp.exp(a_ref[...]) * _softplus(x + dt_ref[...])
    o_ref[0] = jnp.where(lane < nb, beta, g)


def _gdn_gates(p, a_log, dt_bias, col_blk):
    b, lt, _ = p.shape
    nb = a_log.size
    pad = jnp.zeros((LANES,), F32)
    a_row = pad.at[nb:2 * nb].set(a_log.reshape(-1)).reshape(1, LANES)
    dt_row = pad.at[nb:2 * nb].set(dt_bias.reshape(-1)).reshape(1, LANES)
    tr = _tile(lt, 1024, LANES)
    return pl.pallas_call(
        functools.partial(_gdn_gates_kernel, nb=nb),
        grid=(b, lt // tr),
        in_specs=[pl.BlockSpec((1, tr, LANES), lambda i, t: (i, t, col_blk)),
                  pl.BlockSpec((1, LANES), lambda i, t: (0, 0)),
                  pl.BlockSpec((1, LANES), lambda i, t: (0, 0))],
        out_specs=pl.BlockSpec((1, tr, LANES), lambda i, t: (i, t, 0)),
        out_shape=jax.ShapeDtypeStruct((b, lt, LANES), F32),
        compiler_params=_params("parallel", "parallel"),
        name="gdn_gates",
    )(p, a_row, dt_row)


def _block_masks(r, reverse):
    i = lax.broadcasted_iota(jnp.int32, (r, r), 0)
    j = lax.broadcasted_iota(jnp.int32, (r, r), 1)
    same = (i // CHUNK) == (j // CHUNK)
    if reverse:
        return jnp.logical_and(same, i <= j), jnp.logical_and(same, i < j), same
    return jnp.logical_and(same, i >= j), jnp.logical_and(same, i > j), same


def _gdn_local(insts, incl, strict):
    r = insts[0][0].shape[0]
    dv = insts[0][2].shape[1]
    kbs, sols, ms, attns = [], [], [], []
    for q, k, v, beta, gc, gr, tot in insts:
        diff = jnp.broadcast_to(gc, (r, r)) - jnp.broadcast_to(gr, (r, r))
        decay = jnp.where(incl, jnp.exp(jnp.where(incl, diff, 0.0)), 0.0)
        kb = k * beta
        gram = _dot_nt(jnp.concatenate([kb, q], axis=0).astype(BF16), k.astype(BF16))
        ms.append(jnp.where(strict, gram[:r] * decay, 0.0).astype(BF16))
        attns.append((gram[r:] * decay).astype(BF16))
        sols.append(jnp.concatenate([v * beta, kb * jnp.exp(gc)], axis=-1))
    pows = [ms]
    span = 2
    while span < CHUNK:
        pows.append([_dot(p, p).astype(BF16) for p in pows[-1]])
        span *= 2
    for level in reversed(pows[1:]):
        sols = [s + _dot(p, s.astype(BF16)) for p, s in zip(level, sols)]
    sols = [s - _dot(m, s.astype(BF16)) for m, s in zip(ms, sols)]
    aws = [_dot(a, s.astype(BF16)) for a, s in zip(attns, sols)]
    outs = []
    for (q, k, v, beta, gc, gr, tot), sol, aw in zip(insts, sols, aws):
        qe = q * jnp.exp(gc) - aw[:, dv:]
        kd = k * jnp.exp(tot - gc)
        outs.append((sol[:, :dv], sol[:, dv:], qe, kd, aw[:, :dv]))
    return outs


def _gdn_intra_kernel(q_ref, k_ref, v_ref, bg_ref, wq_ref, kd_ref, u_ref, oi_ref, dg_ref, *, nheads, dk, hb):
    r = q_ref.shape[2]
    groups = r // CHUNK
    bg = bg_ref[0]
    for rev in range(2):
        incl, strict, same = _block_masks(r, bool(rev))
        gc_all = _dot01(_as01(incl), bg)
        tot_all = _dot01(_as01(same), bg)
        gr_all = gc_all.T
        for h0 in range(0, nheads, hb):
            insts = []
            for h in range(h0, h0 + hb):
                cols = slice(h * dk, (h + 1) * dk)
                ib, ig = rev * nheads + h, (2 + rev) * nheads + h
                insts.append((q_ref[0, 0, :, cols], k_ref[0, 0, :, cols], v_ref[0, 0, :, cols],
                              bg[:, ib:ib + 1], gc_all[:, ig:ig + 1], gr_all[ig:ig + 1, :],
                              tot_all[:, ig:ig + 1]))
            for h, (u, w, qe, kd, ol) in zip(range(h0, h0 + hb), _gdn_local(insts, incl, strict)):
                cols = slice(h * dk, (h + 1) * dk)
                for g in range(groups):
                    rows = slice(g * CHUNK, (g + 1) * CHUNK)
                    wq_ref[rev, 0, g, 0:CHUNK, cols] = w[rows].astype(BF16)
                    wq_ref[rev, 0, g, CHUNK:2 * CHUNK, cols] = qe[rows].astype(BF16)
                kd_ref[rev, 0, :, cols] = kd.astype(BF16)
                u_ref[rev, 0, :, cols] = u
                if rev == 0:
                    oi_ref[0, :, cols] = ol
                else:
                    oi_ref[0, :, cols] += ol
                dfull = jnp.broadcast_to(jnp.exp(tot_all[:, (2 + rev) * nheads + h:(2 + rev) * nheads + h + 1]),
                                         (r, dk))
                for g in range(groups):
                    dg_ref[rev, 0, g * SUBLANES:(g + 1) * SUBLANES, cols] = dfull[g * CHUNK:g * CHUNK + SUBLANES]


def _gdn_intra(qkv, bg, nheads, dk):
    _, b, lt, width = qkv.shape
    r = _tile(lt, 256, CHUNK)
    groups = r // CHUNK
    nc = lt // CHUNK
    part = lambda c: pl.BlockSpec((1, 1, r, width), lambda i, t: (c, i, t, 0))
    return pl.pallas_call(
        functools.partial(_gdn_intra_kernel, nheads=nheads, dk=dk, hb=4),
        grid=(b, lt // r),
        in_specs=[part(0), part(1), part(2), pl.BlockSpec((1, r, LANES), lambda i, t: (i, t, 0))],
        out_specs=[pl.BlockSpec((2, 1, groups, 2 * CHUNK, width), lambda i, t: (0, i, t, 0, 0)),
                   pl.BlockSpec((2, 1, r, width), lambda i, t: (0, i, t, 0)),
                   pl.BlockSpec((2, 1, r, width), lambda i, t: (0, i, t, 0)),
                   pl.BlockSpec((1, r, width), lambda i, t: (i, t, 0)),
                   pl.BlockSpec((2, 1, groups * SUBLANES, width), lambda i, t: (0, i, t, 0))],
        out_shape=[jax.ShapeDtypeStruct((2, b, nc, 2 * CHUNK, width), BF16),
                   jax.ShapeDtypeStruct((2, b, lt, width), BF16),
                   jax.ShapeDtypeStruct((2, b, lt, width), F32),
                   jax.ShapeDtypeStruct((b, lt, width), F32),
                   jax.ShapeDtypeStruct((2, b, nc * SUBLANES, width), F32)],
        compiler_params=_params("parallel", "parallel"),
        name="gdn_intra",
    )(qkv, qkv, qkv, bg)


def _gdn_scan_kernel(wqf_ref, kdf_ref, uf_ref, dgf_ref, wqb_ref, kdb_ref, ub_ref, dgb_ref, oi_ref,
                     of_ref, ob_ref, s_scr, *, nheads, dk):
    @pl.when(pl.program_id(1) == 0)
    def _():
        s_scr[...] = jnp.zeros_like(s_scr)

    refs = ((wqf_ref, kdf_ref, uf_ref, dgf_ref, of_ref), (wqb_ref, kdb_ref, ub_ref, dgb_ref, ob_ref))
    insts = [(rev, h, slice(h * dk, (h + 1) * dk)) for rev in range(2) for h in range(nheads)]
    states = [s_scr[rev, h] for rev, h, _ in insts]
    ws = [_dot(refs[rev][0][0, 0, 0, :, cols], s.astype(BF16))
          for (rev, h, cols), s in zip(insts, states)]
    v_new = [(refs[rev][2][0, 0, :, cols] - w[:CHUNK]).astype(BF16) for (rev, h, cols), w in zip(insts, ws)]
    upd = [_dot_tn(refs[rev][1][0, 0, :, cols], v) for (rev, h, cols), v in zip(insts, v_new)]
    for (rev, h, cols), s, w, du in zip(insts, states, ws, upd):
        o = w[CHUNK:]
        if rev == 0:
            o = o + oi_ref[0, :, cols]
        refs[rev][4][0, :, cols] = o.astype(BF16)
        s_scr[rev, h] = s * refs[rev][3][0, 0, 0:1, cols] + du


def _chunk_maps(ncc, nc):
    def fwd(n):
        return n

    def bwd(n):
        return jnp.where(n < ncc, ncc - 1 - n, nc - 1 - n + ncc)

    return fwd, bwd


def _gdn_scan(wq, kd, u, dg, oi, lc, nheads, dk):
    _, b, lt, width = kd.shape
    nc, ncc = lt // CHUNK, lc // CHUNK
    fwd, bwd = _chunk_maps(ncc, nc)
    in_specs, args = [], []
    for rev, cm in enumerate((fwd, bwd)):
        in_specs += [
            pl.BlockSpec((1, 1, 1, 2 * CHUNK, width), lambda i, n, cm=cm, rev=rev: (rev, i, cm(n), 0, 0)),
            pl.BlockSpec((1, 1, CHUNK, width), lambda i, n, cm=cm, rev=rev: (rev, i, cm(n), 0)),
            pl.BlockSpec((1, 1, CHUNK, width), lambda i, n, cm=cm, rev=rev: (rev, i, cm(n), 0)),
            pl.BlockSpec((1, 1, SUBLANES, width), lambda i, n, cm=cm, rev=rev: (rev, i, cm(n), 0))]
        args += [wq, kd, u, dg]
    in_specs.append(pl.BlockSpec((1, CHUNK, width), lambda i, n: (i, fwd(n), 0)))
    args.append(oi)
    out_specs = [pl.BlockSpec((1, CHUNK, width), lambda i, n, cm=cm: (i, cm(n), 0)) for cm in (fwd, bwd)]
    shape = jax.ShapeDtypeStruct((b, lt, width), BF16)
    return pl.pallas_call(
        functools.partial(_gdn_scan_kernel, nheads=nheads, dk=dk),
        grid=(b, nc),
        in_specs=in_specs,
        out_specs=out_specs,
        out_shape=[shape, shape],
        scratch_shapes=[pltpu.VMEM((2, nheads, dk, dk), F32)],
        compiler_params=_params("parallel", "arbitrary"),
        name="gdn_scan",
    )(*args)


def _rotate(x, cos, sin):
    half = x.shape[1] // 2
    x1, x2 = x[:, :half], x[:, half:]
    return jnp.concatenate([x1 * cos - x2 * sin, x1 * sin + x2 * cos], axis=-1)


def _lin_stages(insts):
    o1 = [_dot(sc, v) for sc, qn, kn, v, s, dn in insts]
    o2 = [_dot(qn, s.astype(BF16)) for sc, qn, kn, v, s, dn in insts]
    up = [_dot_tn(kn, v) for sc, qn, kn, v, s, dn in insts]
    return [(a + b, s * dn + u) for a, b, u, (sc, qn, kn, v, s, dn) in zip(o1, o2, up, insts)]


def _ret_scan_kernel(qf_ref, kf_ref, vf_ref, cf_ref, sf_ref, qb_ref, kb_ref, vb_ref, cb_ref, sb_ref,
                     ld_ref, of_ref, ob_ref, s_scr, *, dk, dv, hb):
    @pl.when(pl.program_id(2) == 0)
    def _():
        s_scr[...] = jnp.zeros_like(s_scr)

    c = qf_ref.shape[1]
    i = lax.broadcasted_iota(jnp.int32, (c, c), 0)
    j = lax.broadcasted_iota(jnp.int32, (c, c), 1)
    pos = lax.broadcasted_iota(jnp.int32, (c, 1), 0)
    refs = ((qf_ref, kf_ref, vf_ref, cf_ref, sf_ref, of_ref), (qb_ref, kb_ref, vb_ref, cb_ref, sb_ref, ob_ref))
    keys, pre = [], []
    for rev in range(2):
        q_ref, k_ref, v_ref, c_ref, sn_ref, _ = refs[rev]
        cos, sin = c_ref[...], sn_ref[...]
        incl = (i <= j) if rev else (i >= j)
        dist = jnp.abs(i - j).astype(F32)
        steps = ((c - pos) if rev else (pos + 1)).astype(F32)
        for h in range(hb):
            lg = ld_ref[rev, h]
            q = _rotate(q_ref[0, :, h * dk:(h + 1) * dk].astype(F32) * dk ** -0.5, cos, sin)
            k = _rotate(k_ref[0, :, h * dk:(h + 1) * dk].astype(F32), cos, sin)
            decay = jnp.where(incl, jnp.exp(dist * lg[:, :1]), 0.0)
            qn = (q * jnp.exp(steps * lg)).astype(BF16)
            kn = (k * jnp.exp((c - steps) * lg)).astype(BF16)
            keys.append((rev, h))
            pre.append((q.astype(BF16), k.astype(BF16), decay, qn, kn, v_ref[0, :, h * dv:(h + 1) * dv].astype(BF16),
                        s_scr[rev, h], jnp.exp(c * lg[:, :1])))
    raw = [_dot_nt(q, k) for q, k, *_ in pre]
    insts = [((r * decay).astype(BF16), qn, kn, v, s, dn) for r, (q, k, decay, qn, kn, v, s, dn) in zip(raw, pre)]
    for (rev, h), (o, s_new) in zip(keys, _lin_stages(insts)):
        refs[rev][5][0, :, h * dv:(h + 1) * dv] = o.astype(BF16)
        s_scr[rev, h] = s_new


def _ret_scan(p, cos, sin, ld, lc, heads, dk, dv, hb):
    b, lt, _ = p.shape
    c = _tile(math.gcd(lt - lc, lc), 256, CHUNK)
    nc, ncc = lt // c, lc // c
    fwd, bwd = _chunk_maps(ncc, nc)
    half = dk // 2
    k_off = heads // hb
    v_off = 2 * heads * dk // (hb * dv)
    in_specs, args = [], []
    for cm in (fwd, bwd):
        in_specs += [pl.BlockSpec((1, c, hb * dk), lambda i, h, n, cm=cm: (i, cm(n), h)),
                     pl.BlockSpec((1, c, hb * dk), lambda i, h, n, cm=cm: (i, cm(n), k_off + h)),
                     pl.BlockSpec((1, c, hb * dv), lambda i, h, n, cm=cm: (i, cm(n), v_off + h)),
                     pl.BlockSpec((c, half), lambda i, h, n, cm=cm: (cm(n), 0)),
                     pl.BlockSpec((c, half), lambda i, h, n, cm=cm: (cm(n), 0))]
        args += [p, p, p, cos, sin]
    in_specs.append(pl.BlockSpec((2, hb, 1, dk), lambda i, h, n: (0, h, 0, 0)))
    args.append(ld)
    out_specs = [pl.BlockSpec((1, c, hb * dv), lambda i, h, n, cm=cm: (i, cm(n), h)) for cm in (fwd, bwd)]
    shape = jax.ShapeDtypeStruct((b, lt, heads * dv), BF16)
    return pl.pallas_call(
        functools.partial(_ret_scan_kernel, dk=dk, dv=dv, hb=hb),
        grid=(b, heads // hb, nc),
        in_specs=in_specs,
        out_specs=out_specs,
        out_shape=[shape, shape],
        scratch_shapes=[pltpu.VMEM((2, hb, dk, dv), F32)],
        compiler_params=_params("parallel", "parallel", "arbitrary"),
        name="ret_scan",
    )(*args)


def _gla_scan_kernel(qf_ref, kf_ref, vf_ref, lf_ref, qb_ref, kb_ref, vb_ref, lb_ref,
                     gw_ref, gb_ref, of_ref, ob_ref, s_scr, *, heads, dk, dv):
    @pl.when(pl.program_id(1) == 0)
    def _():
        s_scr[...] = jnp.zeros_like(s_scr)

    c = CHUNK
    refs = ((qf_ref, kf_ref, vf_ref, lf_ref, of_ref), (qb_ref, kb_ref, vb_ref, lb_ref, ob_ref))
    ones = jnp.ones((c, LANES), BF16)
    keys, pre = [], []
    for rev in range(2):
        q_ref, k_ref, v_ref, low_ref, _ = refs[rev]
        incl, _, _ = _tri_masks(c, bool(rev))
        logit = _dot(low_ref[0].astype(BF16), gw_ref[rev]) + gb_ref[rev]
        ld = -_softplus(-logit) * (1.0 / GLA_GATE_NORM)
        cum = _dot01(_as01(incl), ld)
        tot = _dot01_tn(ld, ones)
        mid = c // 2 - 1 if rev else c // 2
        ref = cum[mid:mid + 1]
        c_last = cum[0:1] if rev else cum[c - 1:c]
        q = q_ref[0].astype(F32) * dk ** -0.5
        k = k_ref[0].astype(F32)
        qa = (q * jnp.exp(cum - ref)).astype(BF16)
        ka = (k * jnp.exp(ref - cum)).astype(BF16)
        qn = (q * jnp.exp(cum)).astype(BF16)
        kn = (k * jnp.exp(c_last - cum)).astype(BF16)
        for h in range(heads):
            cols = slice(h * dk, (h + 1) * dk)
            dn = jnp.broadcast_to(jnp.exp(tot[h * dk:(h + 1) * dk, :1]), (dk, dv))
            keys.append((rev, h))
            pre.append((qa[:, cols], ka[:, cols], incl, qn[:, cols], kn[:, cols],
                        v_ref[0, :, h * dv:(h + 1) * dv].astype(BF16), s_scr[rev, h], dn))
    raw = [_dot_nt(qa, ka) for qa, ka, *_ in pre]
    insts = [(jnp.where(incl, r, 0.0).astype(BF16), qn, kn, v, s, dn)
             for r, (qa, ka, incl, qn, kn, v, s, dn) in zip(raw, pre)]
    for (rev, h), (o, s_new) in zip(keys, _lin_stages(insts)):
        refs[rev][4][0, :, h * dv:(h + 1) * dv] = o.astype(BF16)
        s_scr[rev, h] = s_new


def _gla_scan(p, low, gw, gb, lc, heads, dk, dv):
    b, lt, _ = p.shape
    nc, ncc = lt // CHUNK, lc // CHUNK
    fwd, bwd = _chunk_maps(ncc, nc)
    qk, vw = heads * dk, heads * dv
    in_specs, args = [], []
    for cm in (fwd, bwd):
        in_specs += [pl.BlockSpec((1, CHUNK, qk), lambda i, n, cm=cm: (i, cm(n), 0)),
                     pl.BlockSpec((1, CHUNK, qk), lambda i, n, cm=cm: (i, cm(n), 1)),
                     pl.BlockSpec((1, CHUNK, vw), lambda i, n, cm=cm: (i, cm(n), 2 * qk // vw)),
                     pl.BlockSpec((1, CHUNK, LANES), lambda i, n, cm=cm: (i, cm(n), 0))]
        args += [p, p, p, low]
    in_specs += [pl.BlockSpec((2, LANES, qk), lambda i, n: (0, 0, 0)),
                 pl.BlockSpec((2, 1, qk), lambda i, n: (0, 0, 0))]
    args += [gw, gb]
    out_specs = [pl.BlockSpec((1, CHUNK, vw), lambda i, n, cm=cm: (i, cm(n), 0)) for cm in (fwd, bwd)]
    shape = jax.ShapeDtypeStruct((b, lt, vw), BF16)
    return pl.pallas_call(
        functools.partial(_gla_scan_kernel, heads=heads, dk=dk, dv=dv),
        grid=(b, nc),
        in_specs=in_specs,
        out_specs=out_specs,
        out_shape=[shape, shape],
        scratch_shapes=[pltpu.VMEM((2, heads, dk, dv), F32)],
        compiler_params=_params("parallel", "arbitrary"),
        name="gla_scan",
    )(*args)


def _axial_tables(l, lc, dim):
    rows = l // GRID_W
    row = jnp.repeat(jnp.arange(rows, dtype=F32), GRID_W)
    col = jnp.tile(jnp.arange(GRID_W, dtype=F32), rows)
    nf = dim // 4
    inv = ROPE_BASE ** (-jnp.arange(nf, dtype=F32) / nf)
    ang = jnp.concatenate([row[:, None] * inv, col[:, None] * inv], axis=-1)
    ang = jnp.concatenate([jnp.zeros((lc, dim // 2), F32), ang], axis=0)
    return jnp.cos(ang), jnp.sin(ang)


def _hy_conv_kernel(pp_ref, pc_ref, pn_ref, w_ref, o_ref, *, tr, lc, lt, taps, row_off):
    row0 = (pl.program_id(1) + row_off) * tr
    o_ref[0, 0] = _conv_rows(pp_ref[0], pc_ref[0], pn_ref[0], w_ref[...], taps, row0, lc, lt).astype(BF16)


def _hy_conv(p, conv_w, lc, d):
    b, lt, _ = p.shape
    l = lt - lc
    taps = conv_w.shape[0]
    tr = _tile(math.gcd(l, lc), 256, LANES)
    row_off = lc // tr
    specs = _halo_specs(tr, d, lambda c: c, row_off, lt)
    return pl.pallas_call(
        functools.partial(_hy_conv_kernel, tr=tr, lc=lc, lt=lt, taps=taps, row_off=row_off),
        grid=(b, l // tr, 3),
        in_specs=specs + [pl.BlockSpec((taps, d), lambda i, t, c: (0, c))],
        out_specs=pl.BlockSpec((1, 1, tr, d), lambda i, t, c: (c, i, t, 0)),
        out_shape=jax.ShapeDtypeStruct((3, b, l, d), BF16),
        compiler_params=_params("parallel", "parallel", "parallel"),
        name="hyena_conv",
    )(p, p, p, conv_w)


def _hy_filter_kernel(z_ref, w1_ref, b1_ref, w2_ref, b2_ref, w3_ref, fr_ref, dl_ref, f_ref, s_ref, *, tr, l):
    t = pl.program_id(0)
    fr = fr_ref[...]
    hid = jnp.sin(fr * (_dot3(z_ref[...], w1_ref[...]) + b1_ref[...]))
    hid = jnp.sin(fr * (_dot3(hid, w2_ref[...]) + b2_ref[...]))
    filt = _dot3(hid, w3_ref[...])
    pos = (t * tr + lax.broadcasted_iota(jnp.int32, (tr, 1), 0)).astype(F32)
    dist = jnp.abs(pos - (l // 2)) / (l // 2)
    filt = filt * jnp.exp(-dist * dl_ref[...])
    f_ref[...] = filt
    part = jnp.sum(jnp.abs(filt), axis=0, keepdims=True)

    @pl.when(t == 0)
    def _():
        s_ref[...] = part

    @pl.when(t > 0)
    def _():
        s_ref[...] += part


def _hy_filter(l, w1, b1, w2, b2, w3, fr, d):
    emb, hf = w1.shape
    n_out = w3.shape[1]
    pos = jnp.arange(l, dtype=F32)
    tt = pos / (l - 1)
    bands = (emb - 1) // 2
    freqs = jnp.linspace(1e-4, bands - 1, bands, dtype=F32)
    ang = (2.0 * math.pi / l) * pos[:, None] * freqs[None, :]
    z = jnp.concatenate([tt[:, None], jnp.cos(ang), -jnp.sin(ang)], axis=-1)
    z = jnp.pad(z, ((0, 0), (0, LANES - emb)))
    w1p = jnp.pad(w1, ((0, LANES - emb), (0, 0)))
    deltas = jnp.abs(jnp.linspace(math.log(HY_TARGET) / HY_SLOW, math.log(HY_TARGET) / HY_FAST, d, dtype=F32))
    dl = jnp.tile(deltas, n_out // d).reshape(1, n_out)
    tr = _tile(l, 256, LANES)
    full = lambda shape: pl.BlockSpec(shape, lambda t: (0,) * len(shape))
    return pl.pallas_call(
        functools.partial(_hy_filter_kernel, tr=tr, l=l),
        grid=(l // tr,),
        in_specs=[pl.BlockSpec((tr, LANES), lambda t: (t, 0)),
                  full((LANES, hf)), full((1, hf)), full((hf, hf)), full((1, hf)), full((hf, n_out)),
                  full((1, hf)), full((1, n_out))],
        out_specs=[pl.BlockSpec((tr, n_out), lambda t: (t, 0)), full((1, n_out))],
        out_shape=[jax.ShapeDtypeStruct((l, n_out), F32), jax.ShapeDtypeStruct((1, n_out), F32)],
        compiler_params=_params("arbitrary"),
        name="hyena_filter",
    )(z, w1p, b1.reshape(1, hf), w2, b2.reshape(1, hf), w3, fr.reshape(1, hf), dl)


def _dft_consts(l):
    n = 2 * l
    n1 = math.isqrt(n)
    assert n1 * n1 == n and n1 % 4 == 0, n
    h = n1 // 2
    idx = np.arange(n1)
    ang = 2.0 * np.pi * np.outer(idx, idx[:h]) / n1
    c, s = np.cos(ang), np.sin(ang)
    fa = np.block([[c, s], [-s, c]])
    ang = 2.0 * np.pi * np.outer(idx[:h] + n1 // 4, idx) / n1
    c, s = np.cos(ang) / n, np.sin(ang) / n
    fc = np.block([[c, -s], [s, c]])
    return n1, jnp.asarray(fa, F32), jnp.asarray(fc, F32)


def _mid_tables(n1):
    n = n1 * n1
    k1 = jnp.arange(n1, dtype=jnp.int32)[:, None, None]
    a = jnp.arange(n1, dtype=jnp.int32)[None, :, None]
    bb = jnp.arange(n1, dtype=jnp.int32)[None, None, :]
    mf = (bb * (n1 * a + k1)) % n
    mi = (a * (n1 * bb + k1)) % n
    out = []
    for m, sign in ((mf, -1.0), (mi, 1.0)):
        ang = m.astype(F32) * (2.0 * math.pi / n)
        c, s = jnp.cos(ang), sign * jnp.sin(ang)
        top = jnp.concatenate([c, -s], axis=2)
        bot = jnp.concatenate([s, c], axis=2)
        out.append(jnp.concatenate([top, bot], axis=1))
    return out


def _fft_a_kernel(z_ref, fa_ref, o_ref, *, n1):
    z = z_ref[...]
    zz = z.reshape(z.shape[0] * z.shape[1], z.shape[2]).astype(BF16)
    out = _dot(fa_ref[...], zz)
    o_ref[0, 0] = out[:n1].astype(BF16)
    o_ref[0, 1] = out[n1:].astype(BF16)


def _fft_a(u, fa, n1, pairs):
    g, h, w = u.shape
    tl = _tile(w, 2048, LANES)
    fa_b = fa[:, :pairs * h].astype(BF16)
    return pl.pallas_call(
        functools.partial(_fft_a_kernel, n1=n1),
        grid=(g // pairs, w // tl),
        in_specs=[pl.BlockSpec((pairs, h, tl), lambda p, j: (p, 0, j)),
                  pl.BlockSpec((2 * n1, pairs * h), lambda p, j: (0, 0))],
        out_specs=pl.BlockSpec((1, 2, n1, tl), lambda p, j: (p, 0, 0, j)),
        out_shape=jax.ShapeDtypeStruct((g // pairs, 2, n1, w), BF16),
        compiler_params=_params("parallel", "parallel"),
        name="fft_first",
    )(u, fa_b)


def _fft_spec_kernel(b_ref, tf_ref, s_ref, o_ref, *, n1):
    x = jnp.concatenate([b_ref[0, 0, 0], b_ref[0, 1, 0]], axis=0).astype(BF16)
    a = _dot(tf_ref[0], x) * (1.0 / s_ref[...])
    o_ref[0, 0] = a[:n1]
    o_ref[1, 0] = a[n1:]


def _fft_spec(bc, tf, norm, n1, dcols):
    bc5 = bc.reshape(1, 2, n1, n1, dcols)
    return pl.pallas_call(
        functools.partial(_fft_spec_kernel, n1=n1),
        grid=(n1,),
        in_specs=[pl.BlockSpec((1, 2, 1, n1, dcols), lambda k: (0, 0, k, 0, 0)),
                  pl.BlockSpec((1, 2 * n1, 2 * n1), lambda k: (k, 0, 0)),
                  pl.BlockSpec((1, dcols), lambda k: (0, 0))],
        out_specs=pl.BlockSpec((2, 1, n1, dcols), lambda k: (0, k, 0, 0)),
        out_shape=jax.ShapeDtypeStruct((2, n1, n1, dcols), F32),
        compiler_params=_params("parallel"),
        name="fft_filter_spectrum",
    )(bc5, tf, norm)


def _fft_mid_kernel(b_ref, tf_ref, ti_ref, f_ref, o_ref, *, n1):
    x = jnp.concatenate([b_ref[0, 0, 0], b_ref[0, 1, 0]], axis=0).astype(BF16)
    a = _dot(tf_ref[0], x)
    ar, ai = a[:n1], a[n1:]
    fr, fi = f_ref[0, 0], f_ref[1, 0]
    y = jnp.concatenate([ar * fr - ai * fi, ar * fi + ai * fr], axis=0).astype(BF16)
    gq = _dot(ti_ref[0], y)
    o_ref[0, 0, 0] = gq[:n1].astype(BF16)
    o_ref[0, 1, 0] = gq[n1:].astype(BF16)


def _fft_mid(bc, tf, ti, spec, order, n1, d):
    npair = bc.shape[0]
    bc5 = bc.reshape(npair, 2, n1, n1, d)
    return pl.pallas_call(
        functools.partial(_fft_mid_kernel, n1=n1),
        grid=(npair, n1),
        in_specs=[pl.BlockSpec((1, 2, 1, n1, d), lambda p, k: (p, 0, k, 0, 0)),
                  pl.BlockSpec((1, 2 * n1, 2 * n1), lambda p, k: (k, 0, 0)),
                  pl.BlockSpec((1, 2 * n1, 2 * n1), lambda p, k: (k, 0, 0)),
                  pl.BlockSpec((2, 1, n1, d), lambda p, k: (0, k, 0, order))],
        out_specs=pl.BlockSpec((1, 2, 1, n1, d), lambda p, k: (p, 0, k, 0, 0)),
        out_shape=jax.ShapeDtypeStruct((npair, 2, n1, n1, d), BF16),
        compiler_params=_params("parallel", "parallel"),
        name="fft_mid",
    )(bc5, tf, ti, spec)


def _fft_c_kernel(g_ref, fc_ref, u_ref, x_ref, sk_ref, o_ref, *, h):
    x = jnp.concatenate([g_ref[0, 0], g_ref[0, 1]], axis=0).astype(BF16)
    y = _dot(fc_ref[...], x)
    reps = u_ref.shape[2] // sk_ref.shape[1]
    skip = jnp.concatenate([sk_ref[...]] * reps, axis=1)
    o_ref[0] = x_ref[0].astype(F32) * (y[:h] + u_ref[0].astype(F32) * skip)
    o_ref[1] = x_ref[1].astype(F32) * (y[h:] + u_ref[1].astype(F32) * skip)


def _fft_c(gc, fc, u, xg, skip, n1, d):
    npair = gc.shape[0]
    g, h, w = u.shape
    gc4 = gc.reshape(npair, 2, n1, w)
    tl = _tile(w, 2048, d)
    return pl.pallas_call(
        functools.partial(_fft_c_kernel, h=h),
        grid=(npair, w // tl),
        in_specs=[pl.BlockSpec((1, 2, n1, tl), lambda p, j: (p, 0, 0, j)),
                  pl.BlockSpec((2 * h, 2 * n1), lambda p, j: (0, 0)),
                  pl.BlockSpec((2, h, tl), lambda p, j: (p, 0, j)),
                  pl.BlockSpec((2, h, tl), lambda p, j: (p, 0, j)),
                  pl.BlockSpec((1, d), lambda p, j: (0, 0))],
        out_specs=pl.BlockSpec((2, h, tl), lambda p, j: (p, 0, j)),
        out_shape=jax.ShapeDtypeStruct((g, h, w), F32),
        compiler_params=_params("parallel", "parallel"),
        name="fft_last",
    )(gc4, fc.astype(BF16), u, xg, skip.reshape(1, d))


def _pad_cols(w, n):
    return jnp.pad(w, ((0, 0), (0, n - w.shape[1])))


def _gdn_layer(xa, modl, g1, lc, w_in, conv_w, a_log, dt_bias, norm_g, w_out):
    d = xa.shape[2]
    dk = 128
    heads = d // dk
    qk = heads * dk
    n_main = 4 * qk
    p, sc = _in_proj(xa, modl, g1, w_in[:, :n_main].astype(BF16), lc, _pad_cols(w_in[:, n_main:], LANES).astype(BF16))
    qkv = _gdn_prep(p, conv_w, lc, qk, dk)
    bg = _gdn_gates(sc, a_log, dt_bias, 0)
    wq, kd, u, oi, dg = _gdn_intra(qkv, bg, heads, dk)
    of, ob = _gdn_scan(wq, kd, u, dg, oi, lc, heads, dk)
    return _out_proj(xa, of, ob, p, 3, jnp.tile(norm_g, heads), w_out.astype(BF16), modl, lc, heads, dk, False)


def _ret_layer(xa, modl, g1, lc, w_in, norm_g, w_out):
    d = xa.shape[2]
    heads, dk, dv = d // 256, 256, 512
    p = _in_proj(xa, modl, g1, w_in.astype(BF16), lc)
    cos, sin = _axial_tables(xa.shape[1] - lc, lc, dk)
    log_gamma = jnp.log1p(-jnp.exp2(-5.0 - jnp.arange(heads, dtype=F32)))
    ld = jnp.stack([log_gamma, log_gamma[::-1]])
    ld = jnp.broadcast_to(ld[:, :, None, None], (2, heads, 1, dk))
    of, ob = _ret_scan(p, cos, sin, ld, lc, heads, dk, dv, 2)
    return _out_proj(xa, of, ob, p, 2, norm_g.reshape(-1), w_out.astype(BF16), modl, lc, heads, dv, True)


def _gla_layer(xa, modl, g1, lc, w_in, gate_w2, gate_b, norm_g, w_out):
    d = xa.shape[2]
    heads = 4
    dk, dv = d // 2 // heads, d // heads
    rank = gate_w2.shape[1]
    n_main = 2 * heads * dk + 2 * heads * dv
    p, low = _in_proj(xa, modl, g1, w_in[:, :n_main].astype(BF16), lc, _pad_cols(w_in[:, n_main:], LANES).astype(BF16))
    gw = jnp.zeros((2, LANES, heads * dk), F32)
    for dr in range(2):
        gw = gw.at[dr, dr * rank:(dr + 1) * rank, :].set(gate_w2[dr])
    gb = gate_b.reshape(2, 1, heads * dk)
    of, ob = _gla_scan(p, low, gw.astype(BF16), gb, lc, heads, dk, dv)
    return _out_proj(xa, of, ob, p, 2, jnp.tile(norm_g, heads), w_out.astype(BF16), modl, lc, heads, dv, False)


def _hyena_layer(xa, modl, g1, lc, w_in, conv_w, w1, b1, w2, b2, w3, fr, skip, w_out):
    b, lt, d = xa.shape
    l = lt - lc
    assert b % 2 == 0
    n1, fa, fc = _dft_consts(l)
    h = n1 // 2
    tf, ti = _mid_tables(n1)
    tf, ti = tf.astype(BF16), ti.astype(BF16)
    order = w3.shape[1] // d

    filt, norm = _hy_filter(l, w1, b1, w2, b2, w3, fr, d)
    fb = _fft_a(filt.reshape(1, h, n1 * order * d), fa, n1, 1)
    spec = _fft_spec(fb, tf, norm, n1, order * d)

    p = _in_proj(xa, modl, g1, w_in.astype(BF16), lc)
    vx = _hy_conv(p, conv_w, lc, d).reshape(3, b, h, n1 * d)
    z = vx[0]
    for o in range(order):
        bc = _fft_a(z, fa, n1, 2)
        gq = _fft_mid(bc, tf, ti, spec, o, n1, d)
        z = _fft_c(gq, fc, z, vx[1 + o], skip[o], n1, d)
    return _plain_out(xa, z.reshape(b, l, d), w_out.astype(BF16), modl, lc)


def kernel(x, c, ctx, c_ctx, ada_w, ada_b, norm1_g, norm2_g, ffn_w1, ffn_w3, ffn_w2, gdn_w_in, gdn_conv_w, gdn_a_log, gdn_dt_bias, gdn_norm_g, gdn_w_out, ret_w_in, ret_norm_g, ret_w_out, gla_w_in, gla_gate_w2, gla_gate_b, gla_norm_g, gla_w_out, hy_w_in, hy_conv_w, hy_ff_w1, hy_ff_b1, hy_ff_w2, hy_ff_b2, hy_ff_w3, hy_sin_freq, hy_skip, hy_w_out, final_norm_g):
    b, l, d = x.shape
    lc = ctx.shape[1]
    depth = ada_w.shape[0]
    assert b < SUBLANES and lc % CHUNK == 0 and l % CHUNK == 0
    xa = jnp.concatenate([ctx, x], axis=1)
    cvec = jnp.zeros((SUBLANES, d), F32).at[:b].set(c).at[b].set(c_ctx)
    mod = _adaln(cvec, ada_w, ada_b).reshape(depth, SUBLANES, 6, d)
    for i in range(depth):
        kind, j = i % N_MIXERS, i // N_MIXERS
        modl, g1 = mod[i], norm1_g[i]
        if kind == 0:
            xa = _gdn_layer(xa, modl, g1, lc, gdn_w_in[j], gdn_conv_w[j], gdn_a_log[j], gdn_dt_bias[j],
                            gdn_norm_g[j], gdn_w_out[j])
        elif kind == 1:
            xa = _ret_layer(xa, modl, g1, lc, ret_w_in[j], ret_norm_g[j], ret_w_out[j])
        elif kind == 2:
            xa = _gla_layer(xa, modl, g1, lc, gla_w_in[j], gla_gate_w2[j], gla_gate_b[j], gla_norm_g[j],
                            gla_w_out[j])
        else:
            xa = _hyena_layer(xa, modl, g1, lc, hy_w_in[j], hy_conv_w[j], hy_ff_w1[j], hy_ff_b1[j], hy_ff_w2[j],
                              hy_ff_b2[j], hy_ff_w3[j], hy_sin_freq[j], hy_skip[j], hy_w_out[j])
        xa = _ffn(xa, modl, norm2_g[i], ffn_w1[i].astype(BF16), ffn_w3[i].astype(BF16), ffn_w2[i].astype(BF16), lc)
    return _final_norm(xa, final_norm_g, lc)
```

```python
import functools
import math

import numpy as np
import jax
import jax.numpy as jnp
from jax import lax
from jax.experimental import pallas as pl
from jax.experimental.pallas import tpu as pltpu

F32 = jnp.float32
BF16 = jnp.bfloat16
EPS = 1e-6
CHUNK = 64
N_MIXERS = 4
GRID_W = 64
ROPE_BASE = 10000.0
GLA_GATE_NORM = 16.0
HY_TARGET, HY_FAST, HY_SLOW = 1e-2, 0.3, 1.5
LANES = 128
SUBLANES = 8
VMEM_LIMIT = 56 * 1024 * 1024


def _params(*sem):
    return pltpu.CompilerParams(dimension_semantics=sem, vmem_limit_bytes=VMEM_LIMIT)


def _tile(total, target, quantum):
    assert total % quantum == 0, (total, quantum)
    best, t = quantum, quantum
    while t <= min(total, target):
        if total % t == 0:
            best = t
        t += quantum
    return best


def _dot(a, b):
    return jnp.dot(a, b, preferred_element_type=F32)


def _dot_nt(a, b):
    return lax.dot_general(a, b, (((1,), (1,)), ((), ())), preferred_element_type=F32)


def _dot_tn(a, b):
    return lax.dot_general(a, b, (((0,), (0,)), ((), ())), preferred_element_type=F32)


def _split2(x):
    hi = x.astype(BF16)
    lo = (x - hi.astype(F32)).astype(BF16)
    return hi, lo


def _split3(x):
    hi = x.astype(BF16)
    r = x - hi.astype(F32)
    mid = r.astype(BF16)
    lo = (r - mid.astype(F32)).astype(BF16)
    return hi, mid, lo


def _dot3(a, b, dot=_dot):
    ah, al = _split2(a)
    bh, bl = _split2(b)
    return dot(ah, bh) + (dot(ah, bl) + dot(al, bh))


def _dot01(m01, x, dot=_dot):
    h, m, l = _split3(x)
    return dot(m01, h) + (dot(m01, m) + dot(m01, l))


def _dot01_tn(x, m01):
    h, m, l = _split3(x)
    return _dot_tn(h, m01) + (_dot_tn(m, m01) + _dot_tn(l, m01))


def _silu(x):
    return x * jax.nn.sigmoid(x)


def _softplus(x):
    return jnp.maximum(x, 0.0) + jnp.log(1.0 + jnp.exp(-jnp.abs(x)))


def _rms(x):
    return x * lax.rsqrt(jnp.mean(x * x, axis=-1, keepdims=True) + EPS)


def _row_select(rows_ctx, a_ctx, a_lat, shape):
    return jnp.where(jnp.broadcast_to(rows_ctx, shape), jnp.broadcast_to(a_ctx, shape),
                     jnp.broadcast_to(a_lat, shape))


def _is_ctx_rows(t, tm, lc):
    rows = t * tm + lax.broadcasted_iota(jnp.int32, (tm, 1), 0)
    return rows < lc


def _norm_mod(x, g, mb, mc, is_ctx, k):
    y = _rms(x) * g
    shift = _row_select(is_ctx, mc[k:k + 1], mb[k:k + 1], x.shape)
    scale = _row_select(is_ctx, mc[k + 1:k + 2], mb[k + 1:k + 2], x.shape)
    return y * (1.0 + scale) + shift


def _tri_masks(c, reverse):
    r = lax.broadcasted_iota(jnp.int32, (c, c), 0)
    q = lax.broadcasted_iota(jnp.int32, (c, c), 1)
    if reverse:
        return r <= q, r < q, r == q
    return r >= q, r > q, r == q


def _as01(mask):
    return jnp.where(mask, 1.0, 0.0).astype(BF16)


def _adaln_kernel(c_ref, w_ref, b_ref, o_ref):
    s = _silu(c_ref[...])
    o_ref[0] = _dot3(s, w_ref[0]) + b_ref[0]


def _adaln(cvec, ada_w, ada_b):
    depth, d, n = ada_w.shape
    tn = _tile(n, 1536, LANES)
    return pl.pallas_call(
        _adaln_kernel,
        grid=(depth, n // tn),
        in_specs=[pl.BlockSpec((SUBLANES, d), lambda i, j: (0, 0)),
                  pl.BlockSpec((1, d, tn), lambda i, j: (i, 0, j)),
                  pl.BlockSpec((1, 1, tn), lambda i, j: (i, 0, j))],
        out_specs=pl.BlockSpec((1, SUBLANES, tn), lambda i, j: (i, 0, j)),
        out_shape=jax.ShapeDtypeStruct((depth, SUBLANES, n), F32),
        compiler_params=_params("parallel", "parallel"),
        name="adaln",
    )(cvec, ada_w, ada_b.reshape(depth, 1, n))


def _resident(shape):
    return pl.BlockSpec(shape, lambda *_: (0,) * len(shape), pipeline_mode=pl.Buffered(1))


def _in_proj_kernel(x_ref, g_ref, mb_ref, mc_ref, w_ref, *rest, tm, lc, tn, tail):
    if tail:
        wt_ref, o_ref, ot_ref, h_scr = rest
    else:
        o_ref, h_scr = rest
    is_ctx = _is_ctx_rows(pl.program_id(1), tm, lc)
    h_scr[...] = _norm_mod(x_ref[0], g_ref[...], mb_ref[0], mc_ref[0], is_ctx, 0).astype(BF16)
    for c in range(w_ref.shape[1] // tn):
        cols = slice(c * tn, (c + 1) * tn)
        o_ref[0, :, cols] = _dot(h_scr[...], w_ref[:, cols]).astype(o_ref.dtype)
    if tail:
        ot_ref[0] = _dot(h_scr[...], wt_ref[...])


def _in_proj(xa, modl, g, w, lc, w_tail=None):
    b, lt, d = xa.shape
    n = w.shape[1]
    tm = _tile(lt, 768, LANES)
    tn = _tile(n, 512, LANES)
    tail = w_tail is not None
    in_specs = [pl.BlockSpec((1, tm, d), lambda i, t: (i, t, 0)),
                pl.BlockSpec((1, d), lambda i, t: (0, 0)),
                pl.BlockSpec((1, 6, d), lambda i, t: (i, 0, 0)),
                pl.BlockSpec((1, 6, d), lambda i, t: (b, 0, 0)),
                _resident((d, n))]
    out_specs = [pl.BlockSpec((1, tm, n), lambda i, t: (i, t, 0))]
    out_shape = [jax.ShapeDtypeStruct((b, lt, n), BF16)]
    args = [xa, g.reshape(1, d), modl, modl, w]
    if tail:
        in_specs.append(_resident((d, LANES)))
        out_specs.append(pl.BlockSpec((1, tm, LANES), lambda i, t: (i, t, 0)))
        out_shape.append(jax.ShapeDtypeStruct((b, lt, LANES), F32))
        args.append(w_tail)
    out = pl.pallas_call(
        functools.partial(_in_proj_kernel, tm=tm, lc=lc, tn=tn, tail=tail),
        grid=(b, lt // tm),
        in_specs=in_specs,
        out_specs=out_specs,
        out_shape=out_shape,
        scratch_shapes=[pltpu.VMEM((tm, d), BF16)],
        compiler_params=_params("parallel", "parallel"),
        name="in_proj",
    )(*args)
    return (out[0], out[1]) if tail else out[0]


def _out_proj_kernel(x_ref, of_ref, ob_ref, gate_ref, ng_ref, w_ref, mb_ref, mc_ref, o_ref,
                     *, heads, dv, center, tm, lc):
    o = of_ref[0].astype(F32) + ob_ref[0].astype(F32)
    parts = []
    for h in range(heads):
        oh = o[:, h * dv:(h + 1) * dv]
        if center:
            oh = oh - jnp.mean(oh, axis=-1, keepdims=True)
        parts.append(_rms(oh))
    on = jnp.concatenate(parts, axis=-1) * ng_ref[...]
    y = (on * _silu(gate_ref[0].astype(F32))).astype(BF16)
    yo = _dot(y, w_ref[...])
    is_ctx = _is_ctx_rows(pl.program_id(1), tm, lc)
    gt = _row_select(is_ctx, mc_ref[0][2:3], mb_ref[0][2:3], yo.shape)
    o_ref[0] = x_ref[0] + gt * yo


def _out_proj(xa, of, ob, p, gate_blk, ng, w, modl, lc, heads, dv, center):
    b, lt, d = xa.shape
    v = heads * dv
    tm = _tile(lt, 384, LANES)
    return pl.pallas_call(
        functools.partial(_out_proj_kernel, heads=heads, dv=dv, center=center, tm=tm, lc=lc),
        grid=(b, lt // tm),
        in_specs=[pl.BlockSpec((1, tm, d), lambda i, t: (i, t, 0)),
                  pl.BlockSpec((1, tm, v), lambda i, t: (i, t, 0)),
                  pl.BlockSpec((1, tm, v), lambda i, t: (i, t, 0)),
                  pl.BlockSpec((1, tm, v), lambda i, t: (i, t, gate_blk)),
                  pl.BlockSpec((1, v), lambda i, t: (0, 0)),
                  pl.BlockSpec((v, d), lambda i, t: (0, 0)),
                  pl.BlockSpec((1, 6, d), lambda i, t: (i, 0, 0)),
                  pl.BlockSpec((1, 6, d), lambda i, t: (b, 0, 0))],
        out_specs=pl.BlockSpec((1, tm, d), lambda i, t: (i, t, 0)),
        out_shape=jax.ShapeDtypeStruct((b, lt, d), F32),
        compiler_params=_params("parallel", "parallel"),
        name="out_proj",
    )(xa, of, ob, p, ng.reshape(1, v), w, modl, modl)


def _plain_out_kernel(x_ref, z_ref, w_ref, mb_ref, o_ref):
    yo = _dot(z_ref[0].astype(BF16), w_ref[...])
    o_ref[0] = x_ref[0] + mb_ref[0][2:3] * yo


def _plain_out(xa, z, w, modl, lc):
    b, lt, d = xa.shape
    l = z.shape[1]
    tm = _tile(math.gcd(l, lc), 256, LANES)
    off = lc // tm
    return pl.pallas_call(
        _plain_out_kernel,
        grid=(b, l // tm),
        in_specs=[pl.BlockSpec((1, tm, d), lambda i, t: (i, t + off, 0)),
                  pl.BlockSpec((1, tm, d), lambda i, t: (i, t, 0)),
                  pl.BlockSpec((d, d), lambda i, t: (0, 0)),
                  pl.BlockSpec((1, 6, d), lambda i, t: (i, 0, 0))],
        out_specs=pl.BlockSpec((1, tm, d), lambda i, t: (i, t + off, 0)),
        out_shape=jax.ShapeDtypeStruct((b, lt, d), F32),
        input_output_aliases={0: 0},
        compiler_params=_params("parallel", "parallel"),
        name="hyena_out",
    )(xa, z, w, modl)


def _ffn_kernel(x_ref, g_ref, mb_ref, mc_ref, w1_ref, w3_ref, w2_ref, o_ref, h_scr, t_scr, *, tm, lc, tf):
    is_ctx = _is_ctx_rows(pl.program_id(1), tm, lc)
    h_scr[...] = _norm_mod(x_ref[0], g_ref[...], mb_ref[0], mc_ref[0], is_ctx, 3).astype(BF16)
    for c in range(w1_ref.shape[1] // tf):
        cols = slice(c * tf, (c + 1) * tf)
        h = h_scr[...]
        t_scr[:, cols] = (_silu(_dot(h, w1_ref[:, cols])) * _dot(h, w3_ref[:, cols])).astype(BF16)
    y = _dot(t_scr[...], w2_ref[...])
    gt = _row_select(is_ctx, mc_ref[0][5:6], mb_ref[0][5:6], y.shape)
    o_ref[0] = x_ref[0] + gt * y


def _ffn(xa, modl, g, w1, w3, w2, lc):
    b, lt, d = xa.shape
    ff = w1.shape[1]
    tm = _tile(lt, 768, LANES)
    tf = _tile(ff, 256, LANES)
    return pl.pallas_call(
        functools.partial(_ffn_kernel, tm=tm, lc=lc, tf=tf),
        grid=(b, lt // tm),
        in_specs=[pl.BlockSpec((1, tm, d), lambda i, t: (i, t, 0)),
                  pl.BlockSpec((1, d), lambda i, t: (0, 0)),
                  pl.BlockSpec((1, 6, d), lambda i, t: (i, 0, 0)),
                  pl.BlockSpec((1, 6, d), lambda i, t: (b, 0, 0)),
                  _resident((d, ff)), _resident((d, ff)), _resident((ff, d))],
        out_specs=pl.BlockSpec((1, tm, d), lambda i, t: (i, t, 0)),
        out_shape=jax.ShapeDtypeStruct((b, lt, d), F32),
        scratch_shapes=[pltpu.VMEM((tm, d), BF16), pltpu.VMEM((tm, ff), BF16)],
        compiler_params=_params("parallel", "parallel"),
        name="ffn",
    )(xa, g.reshape(1, d), modl, modl, w1, w3, w2)


def _final_norm_kernel(x_ref, g_ref, o_ref):
    o_ref[0] = _rms(x_ref[0]) * g_ref[...]


def _final_norm(xa, g, lc):
    b, lt, d = xa.shape
    l = lt - lc
    tm = _tile(math.gcd(l, lc), 512, LANES)
    off = lc // tm
    return pl.pallas_call(
        _final_norm_kernel,
        grid=(b, l // tm),
        in_specs=[pl.BlockSpec((1, tm, d), lambda i, t: (i, t + off, 0)),
                  pl.BlockSpec((1, d), lambda i, t: (0, 0))],
        out_specs=pl.BlockSpec((1, tm, d), lambda i, t: (i, t, 0)),
        out_shape=jax.ShapeDtypeStruct((b, l, d), F32),
        compiler_params=_params("parallel", "parallel"),
        name="final_norm",
    )(xa, g.reshape(1, d))


HALO = 16


def _conv_rows(prev, cur, nxt, w, taps, row0, lc, lt):
    tr = cur.shape[0]
    assert lc % tr == 0 and lt % tr == 0 and taps // 2 <= HALO
    first = jnp.logical_or(row0 == 0, row0 == lc)
    last = jnp.logical_or(row0 + tr == lc, row0 + tr == lt)
    prev = jnp.where(first, 0.0, prev.astype(F32))
    nxt = jnp.where(last, 0.0, nxt.astype(F32))
    win = jnp.concatenate([prev, cur.astype(F32), nxt], axis=0)
    acc = None
    for j in range(taps):
        s = j - taps // 2
        term = win[HALO + s:HALO + s + tr] * w[j:j + 1]
        acc = term if acc is None else acc + term
    return acc


def _halo_specs(tr, width, col_fn, row_off, n_rows):
    r8 = tr // HALO
    last8 = n_rows // HALO - 1

    def prev_map(i, t, c):
        return (i, jnp.maximum((t + row_off) * r8 - 1, 0), col_fn(c))

    def cur_map(i, t, c):
        return (i, t + row_off, col_fn(c))

    def next_map(i, t, c):
        return (i, jnp.minimum((t + row_off + 1) * r8, last8), col_fn(c))

    return [pl.BlockSpec((1, HALO, width), prev_map),
            pl.BlockSpec((1, tr, width), cur_map),
            pl.BlockSpec((1, HALO, width), next_map)]


def _gdn_prep_kernel(pp_ref, pc_ref, pn_ref, w_ref, o_ref, *, tr, lc, lt, taps, dk):
    part = pl.program_id(2)
    row0 = pl.program_id(1) * tr
    u = _silu(_conv_rows(pp_ref[0], pc_ref[0], pn_ref[0], w_ref[...], taps, row0, lc, lt))
    heads = u.shape[1] // dk
    normed = []
    for h in range(heads):
        uh = u[:, h * dk:(h + 1) * dk]
        normed.append(uh * lax.rsqrt(jnp.sum(uh * uh, axis=-1, keepdims=True) + EPS))
    un = jnp.concatenate(normed, axis=-1)
    scale = jnp.where(part == 0, dk ** -0.5, 1.0)
    o_ref[0, 0] = jnp.where(part == 2, u, un * scale)


def _gdn_prep(p, conv_w, lc, qk, dk):
    b, lt, _ = p.shape
    taps = conv_w.shape[0]
    tr = _tile(math.gcd(lt, lc), 256, LANES)
    specs = _halo_specs(tr, qk, lambda c: c, 0, lt)
    return pl.pallas_call(
        functools.partial(_gdn_prep_kernel, tr=tr, lc=lc, lt=lt, taps=taps, dk=dk),
        grid=(b, lt // tr, 3),
        in_specs=specs + [pl.BlockSpec((taps, qk), lambda i, t, c: (0, c))],
        out_specs=pl.BlockSpec((1, 1, tr, qk), lambda i, t, c: (c, i, t, 0)),
        out_shape=jax.ShapeDtypeStruct((3, b, lt, qk), F32),
        compiler_params=_params("parallel", "parallel", "parallel"),
        name="gdn_prep",
    )(p, p, p, conv_w)


def _gdn_gates_kernel(p_ref, a_ref, dt_ref, o_ref, *, nb):
    x = p_ref[0]
    lane = lax.broadcasted_iota(jnp.int32, x.shape, 1)
    beta = jax.nn.sigmoid(x)
    g = -jnp.exp(a_ref[...]) * _softplus(x + dt_ref[...])
    o_ref[0] = jnp.where(lane < nb, beta, g)


def _gdn_gates(p, a_log, dt_bias, col_blk):
    b, lt, _ = p.shape
    nb = a_log.size
    pad = jnp.zeros((LANES,), F32)
    a_row = pad.at[nb:2 * nb].set(a_log.reshape(-1)).reshape(1, LANES)
    dt_row = pad.at[nb:2 * nb].set(dt_bias.reshape(-1)).reshape(1, LANES)
    tr = _tile(lt, 1024, LANES)
    return pl.pallas_call(
        functools.partial(_gdn_gates_kernel, nb=nb),
        grid=(b, lt // tr),
        in_specs=[pl.BlockSpec((1, tr, LANES), lambda i, t: (i, t, col_blk)),
                  pl.BlockSpec((1, LANES), lambda i, t: (0, 0)),
                  pl.BlockSpec((1, LANES), lambda i, t: (0, 0))],
        out_specs=pl.BlockSpec((1, tr, LANES), lambda i, t: (i, t, 0)),
        out_shape=jax.ShapeDtypeStruct((b, lt, LANES), F32),
        compiler_params=_params("parallel", "parallel"),
        name="gdn_gates",
    )(p, a_row, dt_row)


def _block_masks(r, reverse):
    i = lax.broadcasted_iota(jnp.int32, (r, r), 0)
    j = lax.broadcasted_iota(jnp.int32, (r, r), 1)
    same = (i // CHUNK) == (j // CHUNK)
    if reverse:
        return jnp.logical_and(same, i <= j), jnp.logical_and(same, i < j), same
    return jnp.logical_and(same, i >= j), jnp.logical_and(same, i > j), same


def _merge_masks(r):
    i = lax.broadcasted_iota(jnp.int32, (r, r), 0)
    j = lax.broadcasted_iota(jnp.int32, (r, r), 1)
    same = lambda s: (i // s) == (j // s)
    masks, s = [], 2
    while s < CHUNK:
        masks.append(jnp.logical_and(same(2 * s), jnp.logical_not(same(s))))
        s *= 2
    return same(2), masks


def _gdn_local(insts, incl, strict, pair, merge_masks):
    r = insts[0][0].shape[0]
    dv = insts[0][2].shape[1]
    sols, mfs, attns = [], [], []
    for q, k, v, beta, gc, gr, tot in insts:
        diff = jnp.broadcast_to(gc, (r, r)) - jnp.broadcast_to(gr, (r, r))
        decay = jnp.where(incl, jnp.exp(jnp.where(incl, diff, 0.0)), 0.0)
        kb = k * beta
        gram = _dot_nt(jnp.concatenate([kb, q], axis=0).astype(BF16), k.astype(BF16))
        mfs.append(jnp.where(strict, gram[:r] * decay, 0.0))
        attns.append((gram[r:] * decay).astype(BF16))
        sols.append(jnp.concatenate([v * beta, kb * jnp.exp(gc)], axis=-1))
    ns = [-jnp.where(pair, m, 0.0) for m in mfs]
    for off_mask in merge_masks:
        offs = [jnp.where(off_mask, m, 0.0) for m in mfs]
        xs = [o + _dot(n.astype(BF16), o.astype(BF16)) for n, o in zip(ns, offs)]
        ys = [x + _dot(x.astype(BF16), n.astype(BF16)) for x, n in zip(xs, ns)]
        ns = [n - y for n, y in zip(ns, ys)]
    sols = [s + _dot3(n, s) for n, s in zip(ns, sols)]
    aws = [_dot(a, s.astype(BF16)) for a, s in zip(attns, sols)]
    outs = []
    for (q, k, v, beta, gc, gr, tot), sol, aw in zip(insts, sols, aws):
        qe = q * jnp.exp(gc) - aw[:, dv:]
        kd = k * jnp.exp(tot - gc)
        outs.append((sol[:, :dv], sol[:, dv:], qe, kd, aw[:, :dv]))
    return outs


def _gdn_intra_kernel(q_ref, k_ref, v_ref, bg_ref, wq_ref, kd_ref, u_ref, oi_ref, dg_ref, *, nheads, dk, hb):
    r = q_ref.shape[2]
    groups = r // CHUNK
    bg = bg_ref[0]
    o_loc = [None] * nheads
    pair, merge_masks = _merge_masks(r)
    for rev in range(2):
        incl, strict, _ = _block_masks(r, bool(rev))
        gc_all = _dot01(_as01(incl), bg)
        gc3 = gc_all.reshape(groups, CHUNK, LANES)
        edge = gc3[:, 0:1, :] if rev else gc3[:, CHUNK - 1:CHUNK, :]
        tot_all = jnp.broadcast_to(edge, gc3.shape).reshape(r, LANES)
        gr_all = gc_all.T
        for h0 in range(0, nheads, hb):
            insts = []
            for h in range(h0, h0 + hb):
                cols = slice(h * dk, (h + 1) * dk)
                ib, ig = rev * nheads + h, (2 + rev) * nheads + h
                insts.append((q_ref[0, 0, :, cols], k_ref[0, 0, :, cols], v_ref[0, 0, :, cols],
                              bg[:, ib:ib + 1], gc_all[:, ig:ig + 1], gr_all[ig:ig + 1, :],
                              tot_all[:, ig:ig + 1]))
            for h, (u, w, qe, kd, ol) in zip(range(h0, h0 + hb),
                                             _gdn_local(insts, incl, strict, pair, merge_masks)):
                cols = slice(h * dk, (h + 1) * dk)
                for g in range(groups):
                    rows = slice(g * CHUNK, (g + 1) * CHUNK)
                    wq_ref[rev, 0, g, 0:CHUNK, cols] = w[rows].astype(BF16)
                    wq_ref[rev, 0, g, CHUNK:2 * CHUNK, cols] = qe[rows].astype(BF16)
                kd_ref[rev, 0, :, cols] = kd.astype(BF16)
                u_ref[rev, 0, :, cols] = u.astype(BF16)
                if rev == 0:
                    o_loc[h] = ol
                else:
                    oi_ref[0, :, cols] = (o_loc[h] + ol).astype(BF16)
                dfull = jnp.broadcast_to(jnp.exp(tot_all[:, (2 + rev) * nheads + h:(2 + rev) * nheads + h + 1]),
                                         (r, dk))
                for g in range(groups):
                    dg_ref[rev, 0, g * SUBLANES:(g + 1) * SUBLANES, cols] = dfull[g * CHUNK:g * CHUNK + SUBLANES]


def _gdn_intra(qkv, bg, nheads, dk):
    _, b, lt, width = qkv.shape
    r = _tile(lt, 256, CHUNK)
    groups = r // CHUNK
    nc = lt // CHUNK
    part = lambda c: pl.BlockSpec((1, 1, r, width), lambda i, t: (c, i, t, 0))
    return pl.pallas_call(
        functools.partial(_gdn_intra_kernel, nheads=nheads, dk=dk, hb=4),
        grid=(b, lt // r),
        in_specs=[part(0), part(1), part(2), pl.BlockSpec((1, r, LANES), lambda i, t: (i, t, 0))],
        out_specs=[pl.BlockSpec((2, 1, groups, 2 * CHUNK, width), lambda i, t: (0, i, t, 0, 0)),
                   pl.BlockSpec((2, 1, r, width), lambda i, t: (0, i, t, 0)),
                   pl.BlockSpec((2, 1, r, width), lambda i, t: (0, i, t, 0)),
                   pl.BlockSpec((1, r, width), lambda i, t: (i, t, 0)),
                   pl.BlockSpec((2, 1, groups * SUBLANES, width), lambda i, t: (0, i, t, 0))],
        out_shape=[jax.ShapeDtypeStruct((2, b, nc, 2 * CHUNK, width), BF16),
                   jax.ShapeDtypeStruct((2, b, lt, width), BF16),
                   jax.ShapeDtypeStruct((2, b, lt, width), BF16),
                   jax.ShapeDtypeStruct((b, lt, width), BF16),
                   jax.ShapeDtypeStruct((2, b, nc * SUBLANES, width), F32)],
        compiler_params=_params("parallel", "parallel"),
        name="gdn_intra",
    )(qkv, qkv, qkv, bg)


def _gdn_scan_kernel(wqf_ref, kdf_ref, uf_ref, dgf_ref, wqb_ref, kdb_ref, ub_ref, dgb_ref, oi_ref,
                     of_ref, ob_ref, s_scr, *, nheads, dk, cps):
    @pl.when(pl.program_id(1) == 0)
    def _():
        s_scr[...] = jnp.zeros_like(s_scr)

    refs = ((wqf_ref, kdf_ref, uf_ref, dgf_ref, of_ref), (wqb_ref, kdb_ref, ub_ref, dgb_ref, ob_ref))
    insts = [(rev, h, slice(h * dk, (h + 1) * dk)) for rev in range(2) for h in range(nheads)]
    states = [s_scr[rev, h] for rev, h, _ in insts]
    for step in range(cps):
        sub = [cps - 1 - step if rev else step for rev, _, _ in insts]
        ws = [_dot(refs[rev][0][0, 0, c, :, cols], s.astype(BF16))
              for (rev, h, cols), s, c in zip(insts, states, sub)]
        v_new = [(refs[rev][2][0, 0, c * CHUNK:(c + 1) * CHUNK, cols].astype(F32) - w[:CHUNK]).astype(BF16)
                 for (rev, h, cols), w, c in zip(insts, ws, sub)]
        upd = [_dot_tn(refs[rev][1][0, 0, c * CHUNK:(c + 1) * CHUNK, cols], v)
               for (rev, h, cols), v, c in zip(insts, v_new, sub)]
        new_states = []
        for (rev, h, cols), s, w, du, c in zip(insts, states, ws, upd, sub):
            rows = slice(c * CHUNK, (c + 1) * CHUNK)
            o = w[CHUNK:]
            if rev == 0:
                o = o + oi_ref[0, rows, cols].astype(F32)
            refs[rev][4][0, rows, cols] = o.astype(BF16)
            new_states.append(s * refs[rev][3][0, 0, c * SUBLANES:c * SUBLANES + 1, cols] + du)
        states = new_states
    for (rev, h, _), s in zip(insts, states):
        s_scr[rev, h] = s


def _chunk_maps(ncc, nc):
    def fwd(n):
        return n

    def bwd(n):
        return jnp.where(n < ncc, ncc - 1 - n, nc - 1 - n + ncc)

    return fwd, bwd


def _gdn_scan(wq, kd, u, dg, oi, lc, nheads, dk):
    _, b, lt, width = kd.shape
    cps = 2 if (lc // CHUNK) % 2 == 0 and ((lt - lc) // CHUNK) % 2 == 0 else 1
    rows = cps * CHUNK
    nc, ncc = lt // rows, lc // rows
    fwd, bwd = _chunk_maps(ncc, nc)
    in_specs, args = [], []
    for rev, cm in enumerate((fwd, bwd)):
        in_specs += [
            pl.BlockSpec((1, 1, cps, 2 * CHUNK, width), lambda i, n, cm=cm, rev=rev: (rev, i, cm(n), 0, 0)),
            pl.BlockSpec((1, 1, rows, width), lambda i, n, cm=cm, rev=rev: (rev, i, cm(n), 0)),
            pl.BlockSpec((1, 1, rows, width), lambda i, n, cm=cm, rev=rev: (rev, i, cm(n), 0)),
            pl.BlockSpec((1, 1, cps * SUBLANES, width), lambda i, n, cm=cm, rev=rev: (rev, i, cm(n), 0))]
        args += [wq, kd, u, dg]
    in_specs.append(pl.BlockSpec((1, rows, width), lambda i, n: (i, fwd(n), 0)))
    args.append(oi)
    out_specs = [pl.BlockSpec((1, rows, width), lambda i, n, cm=cm: (i, cm(n), 0)) for cm in (fwd, bwd)]
    shape = jax.ShapeDtypeStruct((b, lt, width), BF16)
    return pl.pallas_call(
        functools.partial(_gdn_scan_kernel, nheads=nheads, dk=dk, cps=cps),
        grid=(b, nc),
        in_specs=in_specs,
        out_specs=out_specs,
        out_shape=[shape, shape],
        scratch_shapes=[pltpu.VMEM((2, nheads, dk, dk), F32)],
        compiler_params=_params("parallel", "arbitrary"),
        name="gdn_scan",
    )(*args)


def _rotate(x, cos, sin):
    half = x.shape[1] // 2
    x1, x2 = x[:, :half], x[:, half:]
    return jnp.concatenate([x1 * cos - x2 * sin, x1 * sin + x2 * cos], axis=-1)


def _lin_stages(insts):
    o1 = [_dot(sc, v) for sc, qn, kn, v, s, dn in insts]
    o2 = [_dot(qn, s.astype(BF16)) for sc, qn, kn, v, s, dn in insts]
    up = [_dot_tn(kn, v) for sc, qn, kn, v, s, dn in insts]
    return [(a + b, s * dn + u) for a, b, u, (sc, qn, kn, v, s, dn) in zip(o1, o2, up, insts)]


def _ret_scan_kernel(qf_ref, kf_ref, vf_ref, cf_ref, sf_ref, qb_ref, kb_ref, vb_ref, cb_ref, sb_ref,
                     ld_ref, of_ref, ob_ref, s_scr, *, dk, dv, hb):
    @pl.when(pl.program_id(2) == 0)
    def _():
        s_scr[...] = jnp.zeros_like(s_scr)

    c = qf_ref.shape[1]
    i = lax.broadcasted_iota(jnp.int32, (c, c), 0)
    j = lax.broadcasted_iota(jnp.int32, (c, c), 1)
    pos = lax.broadcasted_iota(jnp.int32, (c, 1), 0)
    refs = ((qf_ref, kf_ref, vf_ref, cf_ref, sf_ref, of_ref), (qb_ref, kb_ref, vb_ref, cb_ref, sb_ref, ob_ref))
    keys, pre = [], []
    for rev in range(2):
        q_ref, k_ref, v_ref, c_ref, sn_ref, _ = refs[rev]
        cos, sin = c_ref[...], sn_ref[...]
        incl = (i <= j) if rev else (i >= j)
        dist = jnp.abs(i - j).astype(F32)
        steps = ((c - pos) if rev else (pos + 1)).astype(F32)
        for h in range(hb):
            lg = ld_ref[rev, h]
            q = _rotate(q_ref[0, :, h * dk:(h + 1) * dk].astype(F32) * dk ** -0.5, cos, sin)
            k = _rotate(k_ref[0, :, h * dk:(h + 1) * dk].astype(F32), cos, sin)
            decay = jnp.where(incl, jnp.exp(dist * lg[:, :1]), 0.0)
            qn = (q * jnp.exp(steps * lg)).astype(BF16)
            kn = (k * jnp.exp((c - steps) * lg)).astype(BF16)
            keys.append((rev, h))
            pre.append((q.astype(BF16), k.astype(BF16), decay, qn, kn, v_ref[0, :, h * dv:(h + 1) * dv].astype(BF16),
                        s_scr[rev, h], jnp.exp(c * lg[:, :1])))
    raw = [_dot_nt(q, k) for q, k, *_ in pre]
    insts = [((r * decay).astype(BF16), qn, kn, v, s, dn) for r, (q, k, decay, qn, kn, v, s, dn) in zip(raw, pre)]
    for (rev, h), (o, s_new) in zip(keys, _lin_stages(insts)):
        refs[rev][5][0, :, h * dv:(h + 1) * dv] = o.astype(BF16)
        s_scr[rev, h] = s_new


def _ret_scan(p, cos, sin, ld, lc, heads, dk, dv, hb):
    b, lt, _ = p.shape
    c = _tile(math.gcd(lt - lc, lc), 256, CHUNK)
    nc, ncc = lt // c, lc // c
    fwd, bwd = _chunk_maps(ncc, nc)
    half = dk // 2
    k_off = heads // hb
    v_off = 2 * heads * dk // (hb * dv)
    in_specs, args = [], []
    for cm in (fwd, bwd):
        in_specs += [pl.BlockSpec((1, c, hb * dk), lambda i, h, n, cm=cm: (i, cm(n), h)),
                     pl.BlockSpec((1, c, hb * dk), lambda i, h, n, cm=cm: (i, cm(n), k_off + h)),
                     pl.BlockSpec((1, c, hb * dv), lambda i, h, n, cm=cm: (i, cm(n), v_off + h)),
                     pl.BlockSpec((c, half), lambda i, h, n, cm=cm: (cm(n), 0)),
                     pl.BlockSpec((c, half), lambda i, h, n, cm=cm: (cm(n), 0))]
        args += [p, p, p, cos, sin]
    in_specs.append(pl.BlockSpec((2, hb, 1, dk), lambda i, h, n: (0, h, 0, 0)))
    args.append(ld)
    out_specs = [pl.BlockSpec((1, c, hb * dv), lambda i, h, n, cm=cm: (i, cm(n), h)) for cm in (fwd, bwd)]
    shape = jax.ShapeDtypeStruct((b, lt, heads * dv), BF16)
    return pl.pallas_call(
        functools.partial(_ret_scan_kernel, dk=dk, dv=dv, hb=hb),
        grid=(b, heads // hb, nc),
        in_specs=in_specs,
        out_specs=out_specs,
        out_shape=[shape, shape],
        scratch_shapes=[pltpu.VMEM((2, hb, dk, dv), F32)],
        compiler_params=_params("parallel", "parallel", "arbitrary"),
        name="ret_scan",
    )(*args)


def _gla_scan_kernel(qf_ref, kf_ref, vf_ref, lf_ref, qb_ref, kb_ref, vb_ref, lb_ref,
                     gw_ref, gb_ref, of_ref, ob_ref, s_scr, *, heads, dk, dv, cps):
    @pl.when(pl.program_id(1) == 0)
    def _():
        s_scr[...] = jnp.zeros_like(s_scr)

    c = CHUNK
    refs = ((qf_ref, kf_ref, vf_ref, lf_ref, of_ref), (qb_ref, kb_ref, vb_ref, lb_ref, ob_ref))
    ones = jnp.ones((c, LANES), BF16)
    keys, pre = [], []
    for rev in range(2):
        q_ref, k_ref, v_ref, low_ref, _ = refs[rev]
        incl, _, _ = _tri_masks(c, bool(rev))
        for sub in range(cps):
            rows = slice(sub * c, (sub + 1) * c)
            logit = _dot(low_ref[0, rows, :].astype(BF16), gw_ref[rev]) + gb_ref[rev]
            ld = -_softplus(-logit) * (1.0 / GLA_GATE_NORM)
            cum = _dot01(_as01(incl), ld)
            tot = _dot01_tn(ld, ones)
            mid = c // 2 - 1 if rev else c // 2
            ref = cum[mid:mid + 1]
            c_last = cum[0:1] if rev else cum[c - 1:c]
            q = q_ref[0, rows, :].astype(F32) * dk ** -0.5
            k = k_ref[0, rows, :].astype(F32)
            qa = (q * jnp.exp(cum - ref)).astype(BF16)
            ka = (k * jnp.exp(ref - cum)).astype(BF16)
            qn = (q * jnp.exp(cum)).astype(BF16)
            kn = (k * jnp.exp(c_last - cum)).astype(BF16)
            for h in range(heads):
                cols = slice(h * dk, (h + 1) * dk)
                dn = jnp.broadcast_to(jnp.exp(tot[h * dk:(h + 1) * dk, :1]), (dk, dv))
                keys.append((rev, h, sub))
                pre.append((qa[:, cols], ka[:, cols], incl, qn[:, cols], kn[:, cols],
                            v_ref[0, rows, h * dv:(h + 1) * dv], dn))
    raw = [_dot_nt(qa, ka) for qa, ka, *_ in pre]
    o1 = [_dot(jnp.where(incl, r, 0.0).astype(BF16), v) for r, (qa, ka, incl, qn, kn, v, dn) in zip(raw, pre)]
    up = [_dot_tn(kn, v) for qa, ka, incl, qn, kn, v, dn in pre]
    local = {key: (qn, dn, a, u) for key, (qa, ka, incl, qn, kn, v, dn), a, u in zip(keys, pre, o1, up)}
    states = {(rev, h): s_scr[rev, h] for rev in range(2) for h in range(heads)}
    for step in range(cps):
        cur = [(rev, h, cps - 1 - step if rev else step) for rev in range(2) for h in range(heads)]
        o2 = [_dot(local[key][0], states[key[:2]].astype(BF16)) for key in cur]
        for (rev, h, sub), b in zip(cur, o2):
            qn, dn, a, u = local[(rev, h, sub)]
            refs[rev][4][0, sub * c:(sub + 1) * c, h * dv:(h + 1) * dv] = (a + b).astype(BF16)
            states[(rev, h)] = states[(rev, h)] * dn + u
    for (rev, h), s in states.items():
        s_scr[rev, h] = s


def _gla_scan(p, low, gw, gb, lc, heads, dk, dv):
    b, lt, _ = p.shape
    cps = 2 if (lc // CHUNK) % 2 == 0 and ((lt - lc) // CHUNK) % 2 == 0 else 1
    rows = cps * CHUNK
    nc, ncc = lt // rows, lc // rows
    fwd, bwd = _chunk_maps(ncc, nc)
    qk, vw = heads * dk, heads * dv
    in_specs, args = [], []
    for cm in (fwd, bwd):
        in_specs += [pl.BlockSpec((1, rows, qk), lambda i, n, cm=cm: (i, cm(n), 0)),
                     pl.BlockSpec((1, rows, qk), lambda i, n, cm=cm: (i, cm(n), 1)),
                     pl.BlockSpec((1, rows, vw), lambda i, n, cm=cm: (i, cm(n), 2 * qk // vw)),
                     pl.BlockSpec((1, rows, LANES), lambda i, n, cm=cm: (i, cm(n), 0))]
        args += [p, p, p, low]
    in_specs += [pl.BlockSpec((2, LANES, qk), lambda i, n: (0, 0, 0)),
                 pl.BlockSpec((2, 1, qk), lambda i, n: (0, 0, 0))]
    args += [gw, gb]
    out_specs = [pl.BlockSpec((1, rows, vw), lambda i, n, cm=cm: (i, cm(n), 0)) for cm in (fwd, bwd)]
    shape = jax.ShapeDtypeStruct((b, lt, vw), BF16)
    return pl.pallas_call(
        functools.partial(_gla_scan_kernel, heads=heads, dk=dk, dv=dv, cps=cps),
        grid=(b, nc),
        in_specs=in_specs,
        out_specs=out_specs,
        out_shape=[shape, shape],
        scratch_shapes=[pltpu.VMEM((2, heads, dk, dv), F32)],
        compiler_params=_params("parallel", "arbitrary"),
        name="gla_scan",
    )(*args)


def _axial_tables(l, lc, dim):
    rows = l // GRID_W
    row = jnp.repeat(jnp.arange(rows, dtype=F32), GRID_W)
    col = jnp.tile(jnp.arange(GRID_W, dtype=F32), rows)
    nf = dim // 4
    inv = ROPE_BASE ** (-jnp.arange(nf, dtype=F32) / nf)
    ang = jnp.concatenate([row[:, None] * inv, col[:, None] * inv], axis=-1)
    ang = jnp.concatenate([jnp.zeros((lc, dim // 2), F32), ang], axis=0)
    return jnp.cos(ang), jnp.sin(ang)


def _hy_conv_kernel(pp_ref, pc_ref, pn_ref, w_ref, o_ref, *, tr, lc, lt, taps, row_off):
    row0 = (pl.program_id(1) + row_off) * tr
    o_ref[0, 0] = _conv_rows(pp_ref[0], pc_ref[0], pn_ref[0], w_ref[...], taps, row0, lc, lt).astype(BF16)


def _hy_conv(p, conv_w, lc, d):
    b, lt, _ = p.shape
    l = lt - lc
    taps = conv_w.shape[0]
    tr = _tile(math.gcd(l, lc), 256, LANES)
    row_off = lc // tr
    specs = _halo_specs(tr, d, lambda c: c, row_off, lt)
    return pl.pallas_call(
        functools.partial(_hy_conv_kernel, tr=tr, lc=lc, lt=lt, taps=taps, row_off=row_off),
        grid=(b, l // tr, 3),
        in_specs=specs + [pl.BlockSpec((taps, d), lambda i, t, c: (0, c))],
        out_specs=pl.BlockSpec((1, 1, tr, d), lambda i, t, c: (c, i, t, 0)),
        out_shape=jax.ShapeDtypeStruct((3, b, l, d), BF16),
        compiler_params=_params("parallel", "parallel", "parallel"),
        name="hyena_conv",
    )(p, p, p, conv_w)


def _hy_filter_kernel(z_ref, w1_ref, b1_ref, w2_ref, b2_ref, w3_ref, fr_ref, dl_ref, f_ref, s_ref, *, tr, l):
    t = pl.program_id(0)
    fr = fr_ref[...]
    hid = jnp.sin(fr * (_dot3(z_ref[...], w1_ref[...]) + b1_ref[...]))
    hid = jnp.sin(fr * (_dot3(hid, w2_ref[...]) + b2_ref[...]))
    filt = _dot3(hid, w3_ref[...])
    pos = (t * tr + lax.broadcasted_iota(jnp.int32, (tr, 1), 0)).astype(F32)
    dist = jnp.abs(pos - (l // 2)) / (l // 2)
    filt = filt * jnp.exp(-dist * dl_ref[...])
    f_ref[...] = filt
    part = jnp.sum(jnp.abs(filt), axis=0, keepdims=True)

    @pl.when(t == 0)
    def _():
        s_ref[...] = part

    @pl.when(t > 0)
    def _():
        s_ref[...] += part


def _hy_filter(l, w1, b1, w2, b2, w3, fr, d):
    emb, hf = w1.shape
    n_out = w3.shape[1]
    pos = jnp.arange(l, dtype=F32)
    tt = pos / (l - 1)
    bands = (emb - 1) // 2
    freqs = jnp.linspace(1e-4, bands - 1, bands, dtype=F32)
    ang = (2.0 * math.pi / l) * pos[:, None] * freqs[None, :]
    z = jnp.concatenate([tt[:, None], jnp.cos(ang), -jnp.sin(ang)], axis=-1)
    z = jnp.pad(z, ((0, 0), (0, LANES - emb)))
    w1p = jnp.pad(w1, ((0, LANES - emb), (0, 0)))
    deltas = jnp.abs(jnp.linspace(math.log(HY_TARGET) / HY_SLOW, math.log(HY_TARGET) / HY_FAST, d, dtype=F32))
    dl = jnp.tile(deltas, n_out // d).reshape(1, n_out)
    tr = _tile(l, 256, LANES)
    full = lambda shape: pl.BlockSpec(shape, lambda t: (0,) * len(shape))
    return pl.pallas_call(
        functools.partial(_hy_filter_kernel, tr=tr, l=l),
        grid=(l // tr,),
        in_specs=[pl.BlockSpec((tr, LANES), lambda t: (t, 0)),
                  full((LANES, hf)), full((1, hf)), full((hf, hf)), full((1, hf)), full((hf, n_out)),
                  full((1, hf)), full((1, n_out))],
        out_specs=[pl.BlockSpec((tr, n_out), lambda t: (t, 0)), full((1, n_out))],
        out_shape=[jax.ShapeDtypeStruct((l, n_out), F32), jax.ShapeDtypeStruct((1, n_out), F32)],
        compiler_params=_params("arbitrary"),
        name="hyena_filter",
    )(z, w1p, b1.reshape(1, hf), w2, b2.reshape(1, hf), w3, fr.reshape(1, hf), dl)


def _dft_consts(l):
    n = 2 * l
    n1 = math.isqrt(n)
    assert n1 * n1 == n and n1 % 4 == 0, n
    h = n1 // 2
    idx = np.arange(n1)
    ang = 2.0 * np.pi * np.outer(idx, idx[:h]) / n1
    c, s = np.cos(ang), np.sin(ang)
    fa = np.block([[c, s], [-s, c]])
    ang = 2.0 * np.pi * np.outer(idx[:h] + n1 // 4, idx) / n1
    c, s = np.cos(ang) / n, np.sin(ang) / n
    fc = np.block([[c, -s], [s, c]])
    return n1, jnp.asarray(fa, F32), jnp.asarray(fc, F32)


def _mid_tables(n1):
    n = n1 * n1
    k1 = jnp.arange(n1, dtype=jnp.int32)[:, None, None]
    a = jnp.arange(n1, dtype=jnp.int32)[None, :, None]
    bb = jnp.arange(n1, dtype=jnp.int32)[None, None, :]
    mf = (bb * (n1 * a + k1)) % n
    mi = (a * (n1 * bb + k1)) % n
    out = []
    for m, sign in ((mf, -1.0), (mi, 1.0)):
        ang = m.astype(F32) * (2.0 * math.pi / n)
        c, s = jnp.cos(ang), sign * jnp.sin(ang)
        top = jnp.concatenate([c, -s], axis=2)
        bot = jnp.concatenate([s, c], axis=2)
        out.append(jnp.concatenate([top, bot], axis=1))
    return out


def _fft_a_kernel(z_ref, fa_ref, o_ref, *, n1):
    z = z_ref[0]
    zz = z.reshape(z.shape[0] * z.shape[1], z.shape[2]).astype(BF16)
    out = _dot(fa_ref[...], zz)
    o_ref[0, 0] = out[:n1].astype(BF16)
    o_ref[0, 1] = out[n1:].astype(BF16)


def _fft_a(u, part, fa, n1, pairs):
    _, g, h, w = u.shape
    tl = _tile(w, 2048, LANES)
    fa_b = fa[:, :pairs * h].astype(BF16)
    return pl.pallas_call(
        functools.partial(_fft_a_kernel, n1=n1),
        grid=(g // pairs, w // tl),
        in_specs=[pl.BlockSpec((1, pairs, h, tl), lambda p, j: (part, p, 0, j)),
                  pl.BlockSpec((2 * n1, pairs * h), lambda p, j: (0, 0))],
        out_specs=pl.BlockSpec((1, 2, n1, tl), lambda p, j: (p, 0, 0, j)),
        out_shape=jax.ShapeDtypeStruct((g // pairs, 2, n1, w), BF16),
        compiler_params=_params("parallel", "parallel"),
        name="fft_first",
    )(u, fa_b)


def _fft_spec_kernel(b_ref, tf_ref, s_ref, o_ref, *, n1):
    x = jnp.concatenate([b_ref[0, 0, 0], b_ref[0, 1, 0]], axis=0).astype(BF16)
    a = _dot(tf_ref[0], x) * (1.0 / s_ref[...])
    o_ref[0, 0] = a[:n1]
    o_ref[1, 0] = a[n1:]


def _fft_spec(bc, tf, norm, n1, dcols):
    bc5 = bc.reshape(1, 2, n1, n1, dcols)
    return pl.pallas_call(
        functools.partial(_fft_spec_kernel, n1=n1),
        grid=(n1,),
        in_specs=[pl.BlockSpec((1, 2, 1, n1, dcols), lambda k: (0, 0, k, 0, 0)),
                  pl.BlockSpec((1, 2 * n1, 2 * n1), lambda k: (k, 0, 0)),
                  pl.BlockSpec((1, dcols), lambda k: (0, 0))],
        out_specs=pl.BlockSpec((2, 1, n1, dcols), lambda k: (0, k, 0, 0)),
        out_shape=jax.ShapeDtypeStruct((2, n1, n1, dcols), F32),
        compiler_params=_params("parallel"),
        name="fft_filter_spectrum",
    )(bc5, tf, norm)


def _fft_mid_kernel(b_ref, tf_ref, ti_ref, f_ref, o_ref, *, n1):
    x = jnp.concatenate([b_ref[0, 0, 0], b_ref[0, 1, 0]], axis=0).astype(BF16)
    a = _dot(tf_ref[0], x)
    ar, ai = a[:n1], a[n1:]
    fr, fi = f_ref[0, 0], f_ref[1, 0]
    y = jnp.concatenate([ar * fr - ai * fi, ar * fi + ai * fr], axis=0).astype(BF16)
    gq = _dot(ti_ref[0], y)
    o_ref[0, 0, 0] = gq[:n1].astype(BF16)
    o_ref[0, 1, 0] = gq[n1:].astype(BF16)


def _fft_mid(bc, tf, ti, spec, order, n1, d):
    npair = bc.shape[0]
    bc5 = bc.reshape(npair, 2, n1, n1, d)
    return pl.pallas_call(
        functools.partial(_fft_mid_kernel, n1=n1),
        grid=(npair, n1),
        in_specs=[pl.BlockSpec((1, 2, 1, n1, d), lambda p, k: (p, 0, k, 0, 0)),
                  pl.BlockSpec((1, 2 * n1, 2 * n1), lambda p, k: (k, 0, 0)),
                  pl.BlockSpec((1, 2 * n1, 2 * n1), lambda p, k: (k, 0, 0)),
                  pl.BlockSpec((2, 1, n1, d), lambda p, k: (0, k, 0, order))],
        out_specs=pl.BlockSpec((1, 2, 1, n1, d), lambda p, k: (p, 0, k, 0, 0)),
        out_shape=jax.ShapeDtypeStruct((npair, 2, n1, n1, d), BF16),
        compiler_params=_params("parallel", "parallel"),
        name="fft_mid",
    )(bc5, tf, ti, spec)


def _fft_c_kernel(g_ref, fc_ref, u_ref, x_ref, sk_ref, o_ref, *, h):
    x = jnp.concatenate([g_ref[0, 0], g_ref[0, 1]], axis=0).astype(BF16)
    y = _dot(fc_ref[...], x)
    reps = u_ref.shape[3] // sk_ref.shape[1]
    skip = jnp.concatenate([sk_ref[...]] * reps, axis=1)
    o_ref[0] = x_ref[0, 0].astype(F32) * (y[:h] + u_ref[0, 0].astype(F32) * skip)
    o_ref[1] = x_ref[0, 1].astype(F32) * (y[h:] + u_ref[0, 1].astype(F32) * skip)


def _fft_c(gc, fc, u, upart, xg, xpart, skip, n1, d):
    npair = gc.shape[0]
    _, g, h, w = u.shape
    gc4 = gc.reshape(npair, 2, n1, w)
    tl = _tile(w, 2048, d)
    return pl.pallas_call(
        functools.partial(_fft_c_kernel, h=h),
        grid=(npair, w // tl),
        in_specs=[pl.BlockSpec((1, 2, n1, tl), lambda p, j: (p, 0, 0, j)),
                  pl.BlockSpec((2 * h, 2 * n1), lambda p, j: (0, 0)),
                  pl.BlockSpec((1, 2, h, tl), lambda p, j: (upart, p, 0, j)),
                  pl.BlockSpec((1, 2, h, tl), lambda p, j: (xpart, p, 0, j)),
                  pl.BlockSpec((1, d), lambda p, j: (0, 0))],
        out_specs=pl.BlockSpec((2, h, tl), lambda p, j: (p, 0, j)),
        out_shape=jax.ShapeDtypeStruct((g, h, w), F32),
        compiler_params=_params("parallel", "parallel"),
        name="fft_last",
    )(gc4, fc.astype(BF16), u, xg, skip.reshape(1, d))


def _pad_cols(w, n):
    return jnp.pad(w, ((0, 0), (0, n - w.shape[1])))


def _gdn_layer(xa, modl, g1, lc, w_in, conv_w, a_log, dt_bias, norm_g, w_out):
    d = xa.shape[2]
    dk = 128
    heads = d // dk
    qk = heads * dk
    n_main = 4 * qk
    p, sc = _in_proj(xa, modl, g1, w_in[:, :n_main].astype(BF16), lc, _pad_cols(w_in[:, n_main:], LANES).astype(BF16))
    qkv = _gdn_prep(p, conv_w, lc, qk, dk)
    bg = _gdn_gates(sc, a_log, dt_bias, 0)
    wq, kd, u, oi, dg = _gdn_intra(qkv, bg, heads, dk)
    of, ob = _gdn_scan(wq, kd, u, dg, oi, lc, heads, dk)
    return _out_proj(xa, of, ob, p, 3, jnp.tile(norm_g, heads), w_out.astype(BF16), modl, lc, heads, dk, False)


def _ret_layer(xa, modl, g1, lc, w_in, norm_g, w_out):
    d = xa.shape[2]
    heads, dk, dv = d // 256, 256, 512
    p = _in_proj(xa, modl, g1, w_in.astype(BF16), lc)
    cos, sin = _axial_tables(xa.shape[1] - lc, lc, dk)
    log_gamma = jnp.log1p(-jnp.exp2(-5.0 - jnp.arange(heads, dtype=F32)))
    ld = jnp.stack([log_gamma, log_gamma[::-1]])
    ld = jnp.broadcast_to(ld[:, :, None, None], (2, heads, 1, dk))
    of, ob = _ret_scan(p, cos, sin, ld, lc, heads, dk, dv, heads)
    return _out_proj(xa, of, ob, p, 2, norm_g.reshape(-1), w_out.astype(BF16), modl, lc, heads, dv, True)


def _gla_layer(xa, modl, g1, lc, w_in, gate_w2, gate_b, norm_g, w_out):
    d = xa.shape[2]
    heads = 4
    dk, dv = d // 2 // heads, d // heads
    rank = gate_w2.shape[1]
    n_main = 2 * heads * dk + 2 * heads * dv
    p, low = _in_proj(xa, modl, g1, w_in[:, :n_main].astype(BF16), lc, _pad_cols(w_in[:, n_main:], LANES).astype(BF16))
    gw = jnp.zeros((2, LANES, heads * dk), F32)
    for dr in range(2):
        gw = gw.at[dr, dr * rank:(dr + 1) * rank, :].set(gate_w2[dr])
    gb = gate_b.reshape(2, 1, heads * dk)
    of, ob = _gla_scan(p, low, gw.astype(BF16), gb, lc, heads, dk, dv)
    return _out_proj(xa, of, ob, p, 2, jnp.tile(norm_g, heads), w_out.astype(BF16), modl, lc, heads, dv, False)


def _hyena_layer(xa, modl, g1, lc, w_in, conv_w, w1, b1, w2, b2, w3, fr, skip, w_out):
    b, lt, d = xa.shape
    l = lt - lc
    assert b % 2 == 0
    n1, fa, fc = _dft_consts(l)
    h = n1 // 2
    tf, ti = _mid_tables(n1)
    tf, ti = tf.astype(BF16), ti.astype(BF16)
    order = w3.shape[1] // d

    filt, norm = _hy_filter(l, w1, b1, w2, b2, w3, fr, d)
    fb = _fft_a(filt.reshape(1, 1, h, n1 * order * d), 0, fa, n1, 1)
    spec = _fft_spec(fb, tf, norm, n1, order * d)

    p = _in_proj(xa, modl, g1, w_in.astype(BF16), lc)
    vx = _hy_conv(p, conv_w, lc, d).reshape(3, b, h, n1 * d)
    u, upart = vx, 0
    for o in range(order):
        bc = _fft_a(u, upart, fa, n1, 2)
        gq = _fft_mid(bc, tf, ti, spec, o, n1, d)
        u, upart = _fft_c(gq, fc, u, upart, vx, 1 + o, skip[o], n1, d)[None], 0
    return _plain_out(xa, u.reshape(b, l, d), w_out.astype(BF16), modl, lc)


def kernel(x, c, ctx, c_ctx, ada_w, ada_b, norm1_g, norm2_g, ffn_w1, ffn_w3, ffn_w2, gdn_w_in, gdn_conv_w, gdn_a_log, gdn_dt_bias, gdn_norm_g, gdn_w_out, ret_w_in, ret_norm_g, ret_w_out, gla_w_in, gla_gate_w2, gla_gate_b, gla_norm_g, gla_w_out, hy_w_in, hy_conv_w, hy_ff_w1, hy_ff_b1, hy_ff_w2, hy_ff_b2, hy_ff_w3, hy_sin_freq, hy_skip, hy_w_out, final_norm_g):
    b, l, d = x.shape
    lc = ctx.shape[1]
    depth = ada_w.shape[0]
    assert b < SUBLANES and lc % CHUNK == 0 and l % CHUNK == 0
    xa = jnp.concatenate([ctx, x], axis=1)
    cvec = jnp.zeros((SUBLANES, d), F32).at[:b].set(c).at[b].set(c_ctx)
    mod = _adaln(cvec, ada_w, ada_b).reshape(depth, SUBLANES, 6, d)
    for i in range(depth):
        kind, j = i % N_MIXERS, i // N_MIXERS
        modl, g1 = mod[i], norm1_g[i]
        if kind == 0:
            xa = _gdn_layer(xa, modl, g1, lc, gdn_w_in[j], gdn_conv_w[j], gdn_a_log[j], gdn_dt_bias[j],
                            gdn_norm_g[j], gdn_w_out[j])
        elif kind == 1:
            xa = _ret_layer(xa, modl, g1, lc, ret_w_in[j], ret_norm_g[j], ret_w_out[j])
        elif kind == 2:
            xa = _gla_layer(xa, modl, g1, lc, gla_w_in[j], gla_gate_w2[j], gla_gate_b[j], gla_norm_g[j],
                            gla_w_out[j])
        else:
            xa = _hyena_layer(xa, modl, g1, lc, hy_w_in[j], hy_conv_w[j], hy_ff_w1[j], hy_ff_b1[j], hy_ff_w2[j],
                              hy_ff_b2[j], hy_ff_w3[j], hy_sin_freq[j], hy_skip[j], hy_w_out[j])
        xa = _ffn(xa, modl, norm2_g[i], ffn_w1[i].astype(BF16), ffn_w3[i].astype(BF16), ffn_w2[i].astype(BF16), lc)
    return _final_norm(xa, final_norm_g, lc)
```

```python
import functools
import math

import numpy as np
import jax
import jax.numpy as jnp
from jax import lax
from jax.experimental import pallas as pl
from jax.experimental.pallas import tpu as pltpu

F32 = jnp.float32
BF16 = jnp.bfloat16
EPS = 1e-6
CHUNK = 64
N_MIXERS = 4
GRID_W = 64
ROPE_BASE = 10000.0
GLA_GATE_NORM = 16.0
HY_TARGET, HY_FAST, HY_SLOW = 1e-2, 0.3, 1.5
LANES = 128
SUBLANES = 8
VMEM_LIMIT = 56 * 1024 * 1024


def _params(*sem):
    return pltpu.CompilerParams(dimension_semantics=sem, vmem_limit_bytes=VMEM_LIMIT)


def _tile(total, target, quantum):
    assert total % quantum == 0, (total, quantum)
    best, t = quantum, quantum
    while t <= min(total, target):
        if total % t == 0:
            best = t
        t += quantum
    return best


def _dot(a, b):
    return jnp.dot(a, b, preferred_element_type=F32)


def _dot_nt(a, b):
    return lax.dot_general(a, b, (((1,), (1,)), ((), ())), preferred_element_type=F32)


def _dot_tn(a, b):
    return lax.dot_general(a, b, (((0,), (0,)), ((), ())), preferred_element_type=F32)


def _split2(x):
    hi = x.astype(BF16)
    lo = (x - hi.astype(F32)).astype(BF16)
    return hi, lo


def _split3(x):
    hi = x.astype(BF16)
    r = x - hi.astype(F32)
    mid = r.astype(BF16)
    lo = (r - mid.astype(F32)).astype(BF16)
    return hi, mid, lo


def _dot3(a, b, dot=_dot):
    ah, al = _split2(a)
    bh, bl = _split2(b)
    return dot(ah, bh) + (dot(ah, bl) + dot(al, bh))


def _dot01(m01, x, dot=_dot):
    h, m, l = _split3(x)
    return dot(m01, h) + (dot(m01, m) + dot(m01, l))


def _dot01_tn(x, m01):
    h, m, l = _split3(x)
    return _dot_tn(h, m01) + (_dot_tn(m, m01) + _dot_tn(l, m01))


def _silu(x):
    return x * jax.nn.sigmoid(x)


def _softplus(x):
    return jnp.maximum(x, 0.0) + jnp.log(1.0 + jnp.exp(-jnp.abs(x)))


def _rms(x):
    return x * lax.rsqrt(jnp.mean(x * x, axis=-1, keepdims=True) + EPS)


def _row_select(rows_ctx, a_ctx, a_lat, shape):
    return jnp.where(jnp.broadcast_to(rows_ctx, shape), jnp.broadcast_to(a_ctx, shape),
                     jnp.broadcast_to(a_lat, shape))


def _is_ctx_rows(t, tm, lc):
    rows = t * tm + lax.broadcasted_iota(jnp.int32, (tm, 1), 0)
    return rows < lc


def _norm_mod(x, g, mb, mc, is_ctx, k):
    y = _rms(x) * g
    shift = _row_select(is_ctx, mc[k:k + 1], mb[k:k + 1], x.shape)
    scale = _row_select(is_ctx, mc[k + 1:k + 2], mb[k + 1:k + 2], x.shape)
    return y * (1.0 + scale) + shift


def _tri_masks(c, reverse):
    r = lax.broadcasted_iota(jnp.int32, (c, c), 0)
    q = lax.broadcasted_iota(jnp.int32, (c, c), 1)
    if reverse:
        return r <= q, r < q, r == q
    return r >= q, r > q, r == q


def _as01(mask):
    return jnp.where(mask, 1.0, 0.0).astype(BF16)


def _adaln_kernel(c_ref, w_ref, b_ref, o_ref):
    s = _silu(c_ref[...])
    o_ref[0] = _dot3(s, w_ref[0]) + b_ref[0]


def _adaln(cvec, ada_w, ada_b):
    depth, d, n = ada_w.shape
    tn = _tile(n, 1536, LANES)
    return pl.pallas_call(
        _adaln_kernel,
        grid=(depth, n // tn),
        in_specs=[pl.BlockSpec((SUBLANES, d), lambda i, j: (0, 0)),
                  pl.BlockSpec((1, d, tn), lambda i, j: (i, 0, j)),
                  pl.BlockSpec((1, 1, tn), lambda i, j: (i, 0, j))],
        out_specs=pl.BlockSpec((1, SUBLANES, tn), lambda i, j: (i, 0, j)),
        out_shape=jax.ShapeDtypeStruct((depth, SUBLANES, n), F32),
        compiler_params=_params("parallel", "parallel"),
        name="adaln",
    )(cvec, ada_w, ada_b.reshape(depth, 1, n))


def _resident(shape):
    return pl.BlockSpec(shape, lambda *_: (0,) * len(shape), pipeline_mode=pl.Buffered(1))


def _in_proj_kernel(x_ref, g_ref, mb_ref, mc_ref, w_ref, *rest, tm, lc, tn, tail):
    if tail:
        wt_ref, o_ref, ot_ref, h_scr = rest
    else:
        o_ref, h_scr = rest
    is_ctx = _is_ctx_rows(pl.program_id(1), tm, lc)
    h_scr[...] = _norm_mod(x_ref[0], g_ref[...], mb_ref[0], mc_ref[0], is_ctx, 0).astype(BF16)
    for c in range(w_ref.shape[1] // tn):
        cols = slice(c * tn, (c + 1) * tn)
        o_ref[0, :, cols] = _dot(h_scr[...], w_ref[:, cols]).astype(o_ref.dtype)
    if tail:
        ot_ref[0] = _dot(h_scr[...], wt_ref[...])


def _in_proj(xa, modl, g, w, lc, w_tail=None):
    b, lt, d = xa.shape
    n = w.shape[1]
    tm = _tile(lt, 768, LANES)
    tn = _tile(n, 512, LANES)
    tail = w_tail is not None
    in_specs = [pl.BlockSpec((1, tm, d), lambda i, t: (i, t, 0)),
                pl.BlockSpec((1, d), lambda i, t: (0, 0)),
                pl.BlockSpec((1, 6, d), lambda i, t: (i, 0, 0)),
                pl.BlockSpec((1, 6, d), lambda i, t: (b, 0, 0)),
                _resident((d, n))]
    out_specs = [pl.BlockSpec((1, tm, n), lambda i, t: (i, t, 0))]
    out_shape = [jax.ShapeDtypeStruct((b, lt, n), BF16)]
    args = [xa, g.reshape(1, d), modl, modl, w]
    if tail:
        in_specs.append(_resident((d, LANES)))
        out_specs.append(pl.BlockSpec((1, tm, LANES), lambda i, t: (i, t, 0)))
        out_shape.append(jax.ShapeDtypeStruct((b, lt, LANES), F32))
        args.append(w_tail)
    out = pl.pallas_call(
        functools.partial(_in_proj_kernel, tm=tm, lc=lc, tn=tn, tail=tail),
        grid=(b, lt // tm),
        in_specs=in_specs,
        out_specs=out_specs,
        out_shape=out_shape,
        scratch_shapes=[pltpu.VMEM((tm, d), BF16)],
        compiler_params=_params("parallel", "parallel"),
        name="in_proj",
    )(*args)
    return (out[0], out[1]) if tail else out[0]


def _out_proj_kernel(x_ref, of_ref, ob_ref, gate_ref, ng_ref, w_ref, mb_ref, mc_ref, o_ref,
                     *, heads, dv, center, tm, lc):
    o = of_ref[0].astype(F32) + ob_ref[0].astype(F32)
    parts = []
    for h in range(heads):
        oh = o[:, h * dv:(h + 1) * dv]
        if center:
            oh = oh - jnp.mean(oh, axis=-1, keepdims=True)
        parts.append(_rms(oh))
    on = jnp.concatenate(parts, axis=-1) * ng_ref[...]
    y = (on * _silu(gate_ref[0].astype(F32))).astype(BF16)
    yo = _dot(y, w_ref[...])
    is_ctx = _is_ctx_rows(pl.program_id(1), tm, lc)
    gt = _row_select(is_ctx, mc_ref[0][2:3], mb_ref[0][2:3], yo.shape)
    o_ref[0] = x_ref[0] + gt * yo


def _out_proj(xa, of, ob, p, gate_blk, ng, w, modl, lc, heads, dv, center):
    b, lt, d = xa.shape
    v = heads * dv
    tm = _tile(lt, 384, LANES)
    return pl.pallas_call(
        functools.partial(_out_proj_kernel, heads=heads, dv=dv, center=center, tm=tm, lc=lc),
        grid=(b, lt // tm),
        in_specs=[pl.BlockSpec((1, tm, d), lambda i, t: (i, t, 0)),
                  pl.BlockSpec((1, tm, v), lambda i, t: (i, t, 0)),
                  pl.BlockSpec((1, tm, v), lambda i, t: (i, t, 0)),
                  pl.BlockSpec((1, tm, v), lambda i, t: (i, t, gate_blk)),
                  pl.BlockSpec((1, v), lambda i, t: (0, 0)),
                  pl.BlockSpec((v, d), lambda i, t: (0, 0)),
                  pl.BlockSpec((1, 6, d), lambda i, t: (i, 0, 0)),
                  pl.BlockSpec((1, 6, d), lambda i, t: (b, 0, 0))],
        out_specs=pl.BlockSpec((1, tm, d), lambda i, t: (i, t, 0)),
        out_shape=jax.ShapeDtypeStruct((b, lt, d), F32),
        compiler_params=_params("parallel", "parallel"),
        name="out_proj",
    )(xa, of, ob, p, ng.reshape(1, v), w, modl, modl)


def _plain_out_kernel(x_ref, z_ref, w_ref, mb_ref, o_ref):
    yo = _dot(z_ref[0].astype(BF16), w_ref[...])
    o_ref[0] = x_ref[0] + mb_ref[0][2:3] * yo


def _plain_out(xa, z, w, modl, lc):
    b, lt, d = xa.shape
    l = z.shape[1]
    tm = _tile(math.gcd(l, lc), 256, LANES)
    off = lc // tm
    return pl.pallas_call(
        _plain_out_kernel,
        grid=(b, l // tm),
        in_specs=[pl.BlockSpec((1, tm, d), lambda i, t: (i, t + off, 0)),
                  pl.BlockSpec((1, tm, d), lambda i, t: (i, t, 0)),
                  pl.BlockSpec((d, d), lambda i, t: (0, 0)),
                  pl.BlockSpec((1, 6, d), lambda i, t: (i, 0, 0))],
        out_specs=pl.BlockSpec((1, tm, d), lambda i, t: (i, t + off, 0)),
        out_shape=jax.ShapeDtypeStruct((b, lt, d), F32),
        input_output_aliases={0: 0},
        compiler_params=_params("parallel", "parallel"),
        name="hyena_out",
    )(xa, z, w, modl)


def _ffn_kernel(x_ref, g_ref, mb_ref, mc_ref, w1_ref, w3_ref, w2_ref, o_ref, h_scr, t_scr, *, tm, lc, tf):
    is_ctx = _is_ctx_rows(pl.program_id(1), tm, lc)
    h_scr[...] = _norm_mod(x_ref[0], g_ref[...], mb_ref[0], mc_ref[0], is_ctx, 3).astype(BF16)
    for c in range(w1_ref.shape[1] // tf):
        cols = slice(c * tf, (c + 1) * tf)
        h = h_scr[...]
        t_scr[:, cols] = (_silu(_dot(h, w1_ref[:, cols])) * _dot(h, w3_ref[:, cols])).astype(BF16)
    y = _dot(t_scr[...], w2_ref[...])
    gt = _row_select(is_ctx, mc_ref[0][5:6], mb_ref[0][5:6], y.shape)
    o_ref[0] = x_ref[0] + gt * y


def _ffn(xa, modl, g, w1, w3, w2, lc):
    b, lt, d = xa.shape
    ff = w1.shape[1]
    tm = _tile(lt, 768, LANES)
    tf = _tile(ff, 256, LANES)
    return pl.pallas_call(
        functools.partial(_ffn_kernel, tm=tm, lc=lc, tf=tf),
        grid=(b, lt // tm),
        in_specs=[pl.BlockSpec((1, tm, d), lambda i, t: (i, t, 0)),
                  pl.BlockSpec((1, d), lambda i, t: (0, 0)),
                  pl.BlockSpec((1, 6, d), lambda i, t: (i, 0, 0)),
                  pl.BlockSpec((1, 6, d), lambda i, t: (b, 0, 0)),
                  _resident((d, ff)), _resident((d, ff)), _resident((ff, d))],
        out_specs=pl.BlockSpec((1, tm, d), lambda i, t: (i, t, 0)),
        out_shape=jax.ShapeDtypeStruct((b, lt, d), F32),
        scratch_shapes=[pltpu.VMEM((tm, d), BF16), pltpu.VMEM((tm, ff), BF16)],
        compiler_params=_params("parallel", "parallel"),
        name="ffn",
    )(xa, g.reshape(1, d), modl, modl, w1, w3, w2)


def _final_norm_kernel(x_ref, g_ref, o_ref):
    o_ref[0] = _rms(x_ref[0]) * g_ref[...]


def _final_norm(xa, g, lc):
    b, lt, d = xa.shape
    l = lt - lc
    tm = _tile(math.gcd(l, lc), 512, LANES)
    off = lc // tm
    return pl.pallas_call(
        _final_norm_kernel,
        grid=(b, l // tm),
        in_specs=[pl.BlockSpec((1, tm, d), lambda i, t: (i, t + off, 0)),
                  pl.BlockSpec((1, d), lambda i, t: (0, 0))],
        out_specs=pl.BlockSpec((1, tm, d), lambda i, t: (i, t, 0)),
        out_shape=jax.ShapeDtypeStruct((b, l, d), F32),
        compiler_params=_params("parallel", "parallel"),
        name="final_norm",
    )(xa, g.reshape(1, d))


HALO = 16


def _conv_rows(prev, cur, nxt, w, taps, row0, lc, lt):
    tr = cur.shape[0]
    assert lc % tr == 0 and lt % tr == 0 and taps // 2 <= HALO
    first = jnp.logical_or(row0 == 0, row0 == lc)
    last = jnp.logical_or(row0 + tr == lc, row0 + tr == lt)
    prev = jnp.where(first, 0.0, prev.astype(F32))
    nxt = jnp.where(last, 0.0, nxt.astype(F32))
    win = jnp.concatenate([prev, cur.astype(F32), nxt], axis=0)
    acc = None
    for j in range(taps):
        s = j - taps // 2
        term = win[HALO + s:HALO + s + tr] * w[j:j + 1]
        acc = term if acc is None else acc + term
    return acc


def _halo_specs(tr, width, col_fn, row_off, n_rows):
    r8 = tr // HALO
    last8 = n_rows // HALO - 1

    def prev_map(i, t, c):
        return (i, jnp.maximum((t + row_off) * r8 - 1, 0), col_fn(c))

    def cur_map(i, t, c):
        return (i, t + row_off, col_fn(c))

    def next_map(i, t, c):
        return (i, jnp.minimum((t + row_off + 1) * r8, last8), col_fn(c))

    return [pl.BlockSpec((1, HALO, width), prev_map),
            pl.BlockSpec((1, tr, width), cur_map),
            pl.BlockSpec((1, HALO, width), next_map)]


def _gdn_prep_kernel(pp_ref, pc_ref, pn_ref, w_ref, o_ref, *, tr, lc, lt, taps, dk):
    part = pl.program_id(2)
    row0 = pl.program_id(1) * tr
    u = _silu(_conv_rows(pp_ref[0], pc_ref[0], pn_ref[0], w_ref[...], taps, row0, lc, lt))
    heads = u.shape[1] // dk
    normed = []
    for h in range(heads):
        uh = u[:, h * dk:(h + 1) * dk]
        normed.append(uh * lax.rsqrt(jnp.sum(uh * uh, axis=-1, keepdims=True) + EPS))
    un = jnp.concatenate(normed, axis=-1)
    scale = jnp.where(part == 0, dk ** -0.5, 1.0)
    o_ref[0, 0] = jnp.where(part == 2, u, un * scale)


def _gdn_prep(p, conv_w, lc, qk, dk):
    b, lt, _ = p.shape
    taps = conv_w.shape[0]
    tr = _tile(math.gcd(lt, lc), 256, LANES)
    specs = _halo_specs(tr, qk, lambda c: c, 0, lt)
    return pl.pallas_call(
        functools.partial(_gdn_prep_kernel, tr=tr, lc=lc, lt=lt, taps=taps, dk=dk),
        grid=(b, lt // tr, 3),
        in_specs=specs + [pl.BlockSpec((taps, qk), lambda i, t, c: (0, c))],
        out_specs=pl.BlockSpec((1, 1, tr, qk), lambda i, t, c: (c, i, t, 0)),
        out_shape=jax.ShapeDtypeStruct((3, b, lt, qk), F32),
        compiler_params=_params("parallel", "parallel", "parallel"),
        name="gdn_prep",
    )(p, p, p, conv_w)


def _gdn_gates_kernel(p_ref, a_ref, dt_ref, o_ref, *, nb):
    x = p_ref[0]
    lane = lax.broadcasted_iota(jnp.int32, x.shape, 1)
    beta = jax.nn.sigmoid(x)
    g = -jnp.exp(a_ref[...]) * _softplus(x + dt_ref[...])
    o_ref[0] = jnp.where(lane < nb, beta, g)


def _gdn_gates(p, a_log, dt_bias, col_blk):
    b, lt, _ = p.shape
    nb = a_log.size
    pad = jnp.zeros((LANES,), F32)
    a_row = pad.at[nb:2 * nb].set(a_log.reshape(-1)).reshape(1, LANES)
    dt_row = pad.at[nb:2 * nb].set(dt_bias.reshape(-1)).reshape(1, LANES)
    tr = _tile(lt, 1024, LANES)
    return pl.pallas_call(
        functools.partial(_gdn_gates_kernel, nb=nb),
        grid=(b, lt // tr),
        in_specs=[pl.BlockSpec((1, tr, LANES), lambda i, t: (i, t, col_blk)),
                  pl.BlockSpec((1, LANES), lambda i, t: (0, 0)),
                  pl.BlockSpec((1, LANES), lambda i, t: (0, 0))],
        out_specs=pl.BlockSpec((1, tr, LANES), lambda i, t: (i, t, 0)),
        out_shape=jax.ShapeDtypeStruct((b, lt, LANES), F32),
        compiler_params=_params("parallel", "parallel"),
        name="gdn_gates",
    )(p, a_row, dt_row)


def _block_masks(r, reverse):
    i = lax.broadcasted_iota(jnp.int32, (r, r), 0)
    j = lax.broadcasted_iota(jnp.int32, (r, r), 1)
    same = (i // CHUNK) == (j // CHUNK)
    if reverse:
        return jnp.logical_and(same, i <= j), jnp.logical_and(same, i < j), same
    return jnp.logical_and(same, i >= j), jnp.logical_and(same, i > j), same


def _merge_masks(r):
    i = lax.broadcasted_iota(jnp.int32, (r, r), 0)
    j = lax.broadcasted_iota(jnp.int32, (r, r), 1)
    same = lambda s: (i // s) == (j // s)
    masks, s = [], 2
    while s < CHUNK:
        masks.append(jnp.logical_and(same(2 * s), jnp.logical_not(same(s))))
        s *= 2
    return same(2), masks


def _gdn_local(insts, incl, strict, pair, merge_masks):
    r = insts[0][0].shape[0]
    dv = insts[0][2].shape[1]
    sols, mfs, attns = [], [], []
    for q, k, v, beta, gc, gr, tot in insts:
        diff = jnp.broadcast_to(gc, (r, r)) - jnp.broadcast_to(gr, (r, r))
        decay = jnp.where(incl, jnp.exp(jnp.where(incl, diff, 0.0)), 0.0)
        kb = k * beta
        gram = _dot_nt(jnp.concatenate([kb, q], axis=0).astype(BF16), k.astype(BF16))
        mfs.append(jnp.where(strict, gram[:r] * decay, 0.0))
        attns.append((gram[r:] * decay).astype(BF16))
        sols.append(jnp.concatenate([v * beta, kb * jnp.exp(gc)], axis=-1))
    ns = [-jnp.where(pair, m, 0.0) for m in mfs]
    for off_mask in merge_masks:
        offs = [jnp.where(off_mask, m, 0.0) for m in mfs]
        xs = [o + _dot(n.astype(BF16), o.astype(BF16)) for n, o in zip(ns, offs)]
        ys = [x + _dot(x.astype(BF16), n.astype(BF16)) for x, n in zip(xs, ns)]
        ns = [n - y for n, y in zip(ns, ys)]
    sols = [s + _dot3(n, s) for n, s in zip(ns, sols)]
    aws = [_dot(a, s.astype(BF16)) for a, s in zip(attns, sols)]
    outs = []
    for (q, k, v, beta, gc, gr, tot), sol, aw in zip(insts, sols, aws):
        qe = q * jnp.exp(gc) - aw[:, dv:]
        kd = k * jnp.exp(tot - gc)
        outs.append((sol[:, :dv], sol[:, dv:], qe, kd, aw[:, :dv]))
    return outs


def _gdn_intra_kernel(q_ref, k_ref, v_ref, bg_ref, wq_ref, kd_ref, u_ref, oi_ref, dg_ref, *, nheads, dk, hb):
    r = q_ref.shape[2]
    groups = r // CHUNK
    bg = bg_ref[0]
    o_loc = [None] * nheads
    pair, merge_masks = _merge_masks(r)
    for rev in range(2):
        incl, strict, _ = _block_masks(r, bool(rev))
        gc_all = _dot01(_as01(incl), bg)
        gc3 = gc_all.reshape(groups, CHUNK, LANES)
        edge = gc3[:, 0:1, :] if rev else gc3[:, CHUNK - 1:CHUNK, :]
        tot_all = jnp.broadcast_to(edge, gc3.shape).reshape(r, LANES)
        gr_all = gc_all.T
        for h0 in range(0, nheads, hb):
            insts = []
            for h in range(h0, h0 + hb):
                cols = slice(h * dk, (h + 1) * dk)
                ib, ig = rev * nheads + h, (2 + rev) * nheads + h
                insts.append((q_ref[0, 0, :, cols], k_ref[0, 0, :, cols], v_ref[0, 0, :, cols],
                              bg[:, ib:ib + 1], gc_all[:, ig:ig + 1], gr_all[ig:ig + 1, :],
                              tot_all[:, ig:ig + 1]))
            for h, (u, w, qe, kd, ol) in zip(range(h0, h0 + hb),
                                             _gdn_local(insts, incl, strict, pair, merge_masks)):
                cols = slice(h * dk, (h + 1) * dk)
                for g in range(groups):
                    rows = slice(g * CHUNK, (g + 1) * CHUNK)
                    wq_ref[rev, 0, g, 0:CHUNK, cols] = w[rows].astype(BF16)
                    wq_ref[rev, 0, g, CHUNK:2 * CHUNK, cols] = qe[rows].astype(BF16)
                kd_ref[rev, 0, :, cols] = kd.astype(BF16)
                u_ref[rev, 0, :, cols] = u.astype(BF16)
                if rev == 0:
                    o_loc[h] = ol
                else:
                    oi_ref[0, :, cols] = (o_loc[h] + ol).astype(BF16)
                dfull = jnp.broadcast_to(jnp.exp(tot_all[:, (2 + rev) * nheads + h:(2 + rev) * nheads + h + 1]),
                                         (r, dk))
                for g in range(groups):
                    dg_ref[rev, 0, g * SUBLANES:(g + 1) * SUBLANES, cols] = dfull[g * CHUNK:g * CHUNK + SUBLANES]


def _gdn_intra(qkv, bg, nheads, dk):
    _, b, lt, width = qkv.shape
    r = _tile(lt, 256, CHUNK)
    groups = r // CHUNK
    nc = lt // CHUNK
    part = lambda c: pl.BlockSpec((1, 1, r, width), lambda i, t: (c, i, t, 0))
    return pl.pallas_call(
        functools.partial(_gdn_intra_kernel, nheads=nheads, dk=dk, hb=4),
        grid=(b, lt // r),
        in_specs=[part(0), part(1), part(2), pl.BlockSpec((1, r, LANES), lambda i, t: (i, t, 0))],
        out_specs=[pl.BlockSpec((2, 1, groups, 2 * CHUNK, width), lambda i, t: (0, i, t, 0, 0)),
                   pl.BlockSpec((2, 1, r, width), lambda i, t: (0, i, t, 0)),
                   pl.BlockSpec((2, 1, r, width), lambda i, t: (0, i, t, 0)),
                   pl.BlockSpec((1, r, width), lambda i, t: (i, t, 0)),
                   pl.BlockSpec((2, 1, groups * SUBLANES, width), lambda i, t: (0, i, t, 0))],
        out_shape=[jax.ShapeDtypeStruct((2, b, nc, 2 * CHUNK, width), BF16),
                   jax.ShapeDtypeStruct((2, b, lt, width), BF16),
                   jax.ShapeDtypeStruct((2, b, lt, width), BF16),
                   jax.ShapeDtypeStruct((b, lt, width), BF16),
                   jax.ShapeDtypeStruct((2, b, nc * SUBLANES, width), F32)],
        compiler_params=_params("parallel", "parallel"),
        name="gdn_intra",
    )(qkv, qkv, qkv, bg)


def _gdn_scan_kernel(wqf_ref, kdf_ref, uf_ref, dgf_ref, wqb_ref, kdb_ref, ub_ref, dgb_ref, oi_ref,
                     of_ref, ob_ref, s_scr, *, nheads, dk, cps):
    @pl.when(pl.program_id(1) == 0)
    def _():
        s_scr[...] = jnp.zeros_like(s_scr)

    refs = ((wqf_ref, kdf_ref, uf_ref, dgf_ref, of_ref), (wqb_ref, kdb_ref, ub_ref, dgb_ref, ob_ref))
    insts = [(rev, h, slice(h * dk, (h + 1) * dk)) for rev in range(2) for h in range(nheads)]
    states = [s_scr[rev, h] for rev, h, _ in insts]
    for step in range(cps):
        sub = [cps - 1 - step if rev else step for rev, _, _ in insts]
        ws = [_dot(refs[rev][0][0, 0, c, :, cols], s.astype(BF16))
              for (rev, h, cols), s, c in zip(insts, states, sub)]
        v_new = [(refs[rev][2][0, 0, c * CHUNK:(c + 1) * CHUNK, cols].astype(F32) - w[:CHUNK]).astype(BF16)
                 for (rev, h, cols), w, c in zip(insts, ws, sub)]
        upd = [_dot_tn(refs[rev][1][0, 0, c * CHUNK:(c + 1) * CHUNK, cols], v)
               for (rev, h, cols), v, c in zip(insts, v_new, sub)]
        new_states = []
        for (rev, h, cols), s, w, du, c in zip(insts, states, ws, upd, sub):
            rows = slice(c * CHUNK, (c + 1) * CHUNK)
            o = w[CHUNK:]
            if rev == 0:
                o = o + oi_ref[0, rows, cols].astype(F32)
            refs[rev][4][0, rows, cols] = o.astype(BF16)
            new_states.append(s * refs[rev][3][0, 0, c * SUBLANES:c * SUBLANES + 1, cols] + du)
        states = new_states
    for (rev, h, _), s in zip(insts, states):
        s_scr[rev, h] = s


def _chunk_maps(ncc, nc):
    def fwd(n):
        return n

    def bwd(n):
        return jnp.where(n < ncc, ncc - 1 - n, nc - 1 - n + ncc)

    return fwd, bwd


def _gdn_scan(wq, kd, u, dg, oi, lc, nheads, dk):
    _, b, lt, width = kd.shape
    cps = 2 if (lc // CHUNK) % 2 == 0 and ((lt - lc) // CHUNK) % 2 == 0 else 1
    rows = cps * CHUNK
    nc, ncc = lt // rows, lc // rows
    fwd, bwd = _chunk_maps(ncc, nc)
    in_specs, args = [], []
    for rev, cm in enumerate((fwd, bwd)):
        in_specs += [
            pl.BlockSpec((1, 1, cps, 2 * CHUNK, width), lambda i, n, cm=cm, rev=rev: (rev, i, cm(n), 0, 0)),
            pl.BlockSpec((1, 1, rows, width), lambda i, n, cm=cm, rev=rev: (rev, i, cm(n), 0)),
            pl.BlockSpec((1, 1, rows, width), lambda i, n, cm=cm, rev=rev: (rev, i, cm(n), 0)),
            pl.BlockSpec((1, 1, cps * SUBLANES, width), lambda i, n, cm=cm, rev=rev: (rev, i, cm(n), 0))]
        args += [wq, kd, u, dg]
    in_specs.append(pl.BlockSpec((1, rows, width), lambda i, n: (i, fwd(n), 0)))
    args.append(oi)
    out_specs = [pl.BlockSpec((1, rows, width), lambda i, n, cm=cm: (i, cm(n), 0)) for cm in (fwd, bwd)]
    shape = jax.ShapeDtypeStruct((b, lt, width), BF16)
    return pl.pallas_call(
        functools.partial(_gdn_scan_kernel, nheads=nheads, dk=dk, cps=cps),
        grid=(b, nc),
        in_specs=in_specs,
        out_specs=out_specs,
        out_shape=[shape, shape],
        scratch_shapes=[pltpu.VMEM((2, nheads, dk, dk), F32)],
        compiler_params=_params("parallel", "arbitrary"),
        name="gdn_scan",
    )(*args)


def _rotate(x, cos, sin):
    half = x.shape[1] // 2
    x1, x2 = x[:, :half], x[:, half:]
    return jnp.concatenate([x1 * cos - x2 * sin, x1 * sin + x2 * cos], axis=-1)


def _lin_stages(insts):
    o1 = [_dot(sc, v) for sc, qn, kn, v, s, dn in insts]
    o2 = [_dot(qn, s.astype(BF16)) for sc, qn, kn, v, s, dn in insts]
    up = [_dot_tn(kn, v) for sc, qn, kn, v, s, dn in insts]
    return [(a + b, s * dn + u) for a, b, u, (sc, qn, kn, v, s, dn) in zip(o1, o2, up, insts)]


def _ret_scan_kernel(qf_ref, kf_ref, vf_ref, cf_ref, sf_ref, qb_ref, kb_ref, vb_ref, cb_ref, sb_ref,
                     ld_ref, of_ref, ob_ref, s_scr, *, dk, dv, hb):
    @pl.when(pl.program_id(2) == 0)
    def _():
        s_scr[...] = jnp.zeros_like(s_scr)

    c = qf_ref.shape[1]
    i = lax.broadcasted_iota(jnp.int32, (c, c), 0)
    j = lax.broadcasted_iota(jnp.int32, (c, c), 1)
    pos = lax.broadcasted_iota(jnp.int32, (c, 1), 0)
    refs = ((qf_ref, kf_ref, vf_ref, cf_ref, sf_ref, of_ref), (qb_ref, kb_ref, vb_ref, cb_ref, sb_ref, ob_ref))
    keys, pre = [], []
    for rev in range(2):
        q_ref, k_ref, v_ref, c_ref, sn_ref, _ = refs[rev]
        cos, sin = c_ref[...], sn_ref[...]
        incl = (i <= j) if rev else (i >= j)
        dist = jnp.abs(i - j).astype(F32)
        steps = ((c - pos) if rev else (pos + 1)).astype(F32)
        for h in range(hb):
            lg = ld_ref[rev, h]
            q = _rotate(q_ref[0, :, h * dk:(h + 1) * dk].astype(F32) * dk ** -0.5, cos, sin)
            k = _rotate(k_ref[0, :, h * dk:(h + 1) * dk].astype(F32), cos, sin)
            decay = jnp.where(incl, jnp.exp(dist * lg[:, :1]), 0.0)
            qn = (q * jnp.exp(steps * lg)).astype(BF16)
            kn = (k * jnp.exp((c - steps) * lg)).astype(BF16)
            keys.append((rev, h))
            pre.append((q.astype(BF16), k.astype(BF16), decay, qn, kn, v_ref[0, :, h * dv:(h + 1) * dv].astype(BF16),
                        s_scr[rev, h], jnp.exp(c * lg[:, :1])))
    raw = [_dot_nt(q, k) for q, k, *_ in pre]
    insts = [((r * decay).astype(BF16), qn, kn, v, s, dn) for r, (q, k, decay, qn, kn, v, s, dn) in zip(raw, pre)]
    for (rev, h), (o, s_new) in zip(keys, _lin_stages(insts)):
        refs[rev][5][0, :, h * dv:(h + 1) * dv] = o.astype(BF16)
        s_scr[rev, h] = s_new


def _ret_scan(p, cos, sin, ld, lc, heads, dk, dv, hb):
    b, lt, _ = p.shape
    c = _tile(math.gcd(lt - lc, lc), 256, CHUNK)
    nc, ncc = lt // c, lc // c
    fwd, bwd = _chunk_maps(ncc, nc)
    half = dk // 2
    k_off = heads // hb
    v_off = 2 * heads * dk // (hb * dv)
    in_specs, args = [], []
    for cm in (fwd, bwd):
        in_specs += [pl.BlockSpec((1, c, hb * dk), lambda i, h, n, cm=cm: (i, cm(n), h)),
                     pl.BlockSpec((1, c, hb * dk), lambda i, h, n, cm=cm: (i, cm(n), k_off + h)),
                     pl.BlockSpec((1, c, hb * dv), lambda i, h, n, cm=cm: (i, cm(n), v_off + h)),
                     pl.BlockSpec((c, half), lambda i, h, n, cm=cm: (cm(n), 0)),
                     pl.BlockSpec((c, half), lambda i, h, n, cm=cm: (cm(n), 0))]
        args += [p, p, p, cos, sin]
    in_specs.append(pl.BlockSpec((2, hb, 1, dk), lambda i, h, n: (0, h, 0, 0)))
    args.append(ld)
    out_specs = [pl.BlockSpec((1, c, hb * dv), lambda i, h, n, cm=cm: (i, cm(n), h)) for cm in (fwd, bwd)]
    shape = jax.ShapeDtypeStruct((b, lt, heads * dv), BF16)
    return pl.pallas_call(
        functools.partial(_ret_scan_kernel, dk=dk, dv=dv, hb=hb),
        grid=(b, heads // hb, nc),
        in_specs=in_specs,
        out_specs=out_specs,
        out_shape=[shape, shape],
        scratch_shapes=[pltpu.VMEM((2, hb, dk, dv), F32)],
        compiler_params=_params("parallel", "parallel", "arbitrary"),
        name="ret_scan",
    )(*args)


def _gla_scan_kernel(qf_ref, kf_ref, vf_ref, lf_ref, qb_ref, kb_ref, vb_ref, lb_ref,
                     gw_ref, gb_ref, of_ref, ob_ref, s_scr, *, heads, dk, dv, cps):
    @pl.when(pl.program_id(1) == 0)
    def _():
        s_scr[...] = jnp.zeros_like(s_scr)

    c = CHUNK
    refs = ((qf_ref, kf_ref, vf_ref, lf_ref, of_ref), (qb_ref, kb_ref, vb_ref, lb_ref, ob_ref))
    ones = jnp.ones((c, LANES), BF16)
    keys, pre = [], []
    for rev in range(2):
        q_ref, k_ref, v_ref, low_ref, _ = refs[rev]
        incl, _, _ = _tri_masks(c, bool(rev))
        for sub in range(cps):
            rows = slice(sub * c, (sub + 1) * c)
            logit = _dot(low_ref[0, rows, :].astype(BF16), gw_ref[rev]) + gb_ref[rev]
            ld = -_softplus(-logit) * (1.0 / GLA_GATE_NORM)
            cum = _dot01(_as01(incl), ld)
            tot = _dot01_tn(ld, ones)
            mid = c // 2 - 1 if rev else c // 2
            ref = cum[mid:mid + 1]
            c_last = cum[0:1] if rev else cum[c - 1:c]
            q = q_ref[0, rows, :].astype(F32) * dk ** -0.5
            k = k_ref[0, rows, :].astype(F32)
            qa = (q * jnp.exp(cum - ref)).astype(BF16)
            ka = (k * jnp.exp(ref - cum)).astype(BF16)
            qn = (q * jnp.exp(cum)).astype(BF16)
            kn = (k * jnp.exp(c_last - cum)).astype(BF16)
            for h in range(heads):
                cols = slice(h * dk, (h + 1) * dk)
                dn = jnp.broadcast_to(jnp.exp(tot[h * dk:(h + 1) * dk, :1]), (dk, dv))
                keys.append((rev, h, sub))
                pre.append((qa[:, cols], ka[:, cols], incl, qn[:, cols], kn[:, cols],
                            v_ref[0, rows, h * dv:(h + 1) * dv], dn))
    raw = [_dot_nt(qa, ka) for qa, ka, *_ in pre]
    o1 = [_dot(jnp.where(incl, r, 0.0).astype(BF16), v) for r, (qa, ka, incl, qn, kn, v, dn) in zip(raw, pre)]
    up = [_dot_tn(kn, v) for qa, ka, incl, qn, kn, v, dn in pre]
    local = {key: (qn, dn, a, u) for key, (qa, ka, incl, qn, kn, v, dn), a, u in zip(keys, pre, o1, up)}
    states = {(rev, h): s_scr[rev, h] for rev in range(2) for h in range(heads)}
    for step in range(cps):
        cur = [(rev, h, cps - 1 - step if rev else step) for rev in range(2) for h in range(heads)]
        o2 = [_dot(local[key][0], states[key[:2]].astype(BF16)) for key in cur]
        for (rev, h, sub), b in zip(cur, o2):
            qn, dn, a, u = local[(rev, h, sub)]
            refs[rev][4][0, sub * c:(sub + 1) * c, h * dv:(h + 1) * dv] = (a + b).astype(BF16)
            states[(rev, h)] = states[(rev, h)] * dn + u
    for (rev, h), s in states.items():
        s_scr[rev, h] = s


def _gla_scan(p, low, gw, gb, lc, heads, dk, dv):
    b, lt, _ = p.shape
    cps = 2 if (lc // CHUNK) % 2 == 0 and ((lt - lc) // CHUNK) % 2 == 0 else 1
    rows = cps * CHUNK
    nc, ncc = lt // rows, lc // rows
    fwd, bwd = _chunk_maps(ncc, nc)
    qk, vw = heads * dk, heads * dv
    in_specs, args = [], []
    for cm in (fwd, bwd):
        in_specs += [pl.BlockSpec((1, rows, qk), lambda i, n, cm=cm: (i, cm(n), 0)),
                     pl.BlockSpec((1, rows, qk), lambda i, n, cm=cm: (i, cm(n), 1)),
                     pl.BlockSpec((1, rows, vw), lambda i, n, cm=cm: (i, cm(n), 2 * qk // vw)),
                     pl.BlockSpec((1, rows, LANES), lambda i, n, cm=cm: (i, cm(n), 0))]
        args += [p, p, p, low]
    in_specs += [pl.BlockSpec((2, LANES, qk), lambda i, n: (0, 0, 0)),
                 pl.BlockSpec((2, 1, qk), lambda i, n: (0, 0, 0))]
    args += [gw, gb]
    out_specs = [pl.BlockSpec((1, rows, vw), lambda i, n, cm=cm: (i, cm(n), 0)) for cm in (fwd, bwd)]
    shape = jax.ShapeDtypeStruct((b, lt, vw), BF16)
    return pl.pallas_call(
        functools.partial(_gla_scan_kernel, heads=heads, dk=dk, dv=dv, cps=cps),
        grid=(b, nc),
        in_specs=in_specs,
        out_specs=out_specs,
        out_shape=[shape, shape],
        scratch_shapes=[pltpu.VMEM((2, heads, dk, dv), F32)],
        compiler_params=_params("parallel", "arbitrary"),
        name="gla_scan",
    )(*args)


def _axial_tables(l, lc, dim):
    rows = l // GRID_W
    row = jnp.repeat(jnp.arange(rows, dtype=F32), GRID_W)
    col = jnp.tile(jnp.arange(GRID_W, dtype=F32), rows)
    nf = dim // 4
    inv = ROPE_BASE ** (-jnp.arange(nf, dtype=F32) / nf)
    ang = jnp.concatenate([row[:, None] * inv, col[:, None] * inv], axis=-1)
    ang = jnp.concatenate([jnp.zeros((lc, dim // 2), F32), ang], axis=0)
    return jnp.cos(ang), jnp.sin(ang)


def _hy_conv_kernel(pp_ref, pc_ref, pn_ref, w_ref, o_ref, *, tr, lc, lt, taps, row_off):
    row0 = (pl.program_id(1) + row_off) * tr
    o_ref[0, 0] = _conv_rows(pp_ref[0], pc_ref[0], pn_ref[0], w_ref[...], taps, row0, lc, lt).astype(BF16)


def _hy_conv(p, conv_w, lc, d):
    b, lt, _ = p.shape
    l = lt - lc
    taps = conv_w.shape[0]
    tr = _tile(math.gcd(l, lc), 256, LANES)
    row_off = lc // tr
    specs = _halo_specs(tr, d, lambda c: c, row_off, lt)
    return pl.pallas_call(
        functools.partial(_hy_conv_kernel, tr=tr, lc=lc, lt=lt, taps=taps, row_off=row_off),
        grid=(b, l // tr, 3),
        in_specs=specs + [pl.BlockSpec((taps, d), lambda i, t, c: (0, c))],
        out_specs=pl.BlockSpec((1, 1, tr, d), lambda i, t, c: (c, i, t, 0)),
        out_shape=jax.ShapeDtypeStruct((3, b, l, d), BF16),
        compiler_params=_params("parallel", "parallel", "parallel"),
        name="hyena_conv",
    )(p, p, p, conv_w)


def _hy_filter_kernel(z_ref, w1_ref, b1_ref, w2_ref, b2_ref, w3_ref, fr_ref, dl_ref, f_ref, s_ref, *, tr, l):
    t = pl.program_id(0)
    fr = fr_ref[...]
    hid = jnp.sin(fr * (_dot3(z_ref[...], w1_ref[...]) + b1_ref[...]))
    hid = jnp.sin(fr * (_dot3(hid, w2_ref[...]) + b2_ref[...]))
    filt = _dot3(hid, w3_ref[...])
    pos = (t * tr + lax.broadcasted_iota(jnp.int32, (tr, 1), 0)).astype(F32)
    dist = jnp.abs(pos - (l // 2)) / (l // 2)
    filt = filt * jnp.exp(-dist * dl_ref[...])
    f_ref[...] = filt
    part = jnp.sum(jnp.abs(filt), axis=0, keepdims=True)

    @pl.when(t == 0)
    def _():
        s_ref[...] = part

    @pl.when(t > 0)
    def _():
        s_ref[...] += part


def _hy_filter(l, w1, b1, w2, b2, w3, fr, d):
    emb, hf = w1.shape
    n_out = w3.shape[1]
    pos = jnp.arange(l, dtype=F32)
    tt = pos / (l - 1)
    bands = (emb - 1) // 2
    freqs = jnp.linspace(1e-4, bands - 1, bands, dtype=F32)
    ang = (2.0 * math.pi / l) * pos[:, None] * freqs[None, :]
    z = jnp.concatenate([tt[:, None], jnp.cos(ang), -jnp.sin(ang)], axis=-1)
    z = jnp.pad(z, ((0, 0), (0, LANES - emb)))
    w1p = jnp.pad(w1, ((0, LANES - emb), (0, 0)))
    deltas = jnp.abs(jnp.linspace(math.log(HY_TARGET) / HY_SLOW, math.log(HY_TARGET) / HY_FAST, d, dtype=F32))
    dl = jnp.tile(deltas, n_out // d).reshape(1, n_out)
    tr = _tile(l, 256, LANES)
    full = lambda shape: pl.BlockSpec(shape, lambda t: (0,) * len(shape))
    return pl.pallas_call(
        functools.partial(_hy_filter_kernel, tr=tr, l=l),
        grid=(l // tr,),
        in_specs=[pl.BlockSpec((tr, LANES), lambda t: (t, 0)),
                  full((LANES, hf)), full((1, hf)), full((hf, hf)), full((1, hf)), full((hf, n_out)),
                  full((1, hf)), full((1, n_out))],
        out_specs=[pl.BlockSpec((tr, n_out), lambda t: (t, 0)), full((1, n_out))],
        out_shape=[jax.ShapeDtypeStruct((l, n_out), F32), jax.ShapeDtypeStruct((1, n_out), F32)],
        compiler_params=_params("arbitrary"),
        name="hyena_filter",
    )(z, w1p, b1.reshape(1, hf), w2, b2.reshape(1, hf), w3, fr.reshape(1, hf), dl)


def _dft_consts(l):
    n = 2 * l
    n1 = math.isqrt(n)
    assert n1 * n1 == n and n1 % 4 == 0, n
    h = n1 // 2
    idx = np.arange(n1)
    ang = 2.0 * np.pi * np.outer(idx, idx[:h]) / n1
    c, s = np.cos(ang), np.sin(ang)
    fa = np.block([[c, s], [-s, c]])
    ang = 2.0 * np.pi * np.outer(idx[:h] + n1 // 4, idx) / n1
    c, s = np.cos(ang) / n, np.sin(ang) / n
    fc = np.block([[c, -s], [s, c]])
    return n1, jnp.asarray(fa, F32), jnp.asarray(fc, F32)


def _mid_tables(n1):
    n = n1 * n1
    k1 = jnp.arange(n1, dtype=jnp.int32)[:, None, None]
    a = jnp.arange(n1, dtype=jnp.int32)[None, :, None]
    bb = jnp.arange(n1, dtype=jnp.int32)[None, None, :]
    mf = (bb * (n1 * a + k1)) % n
    mi = (a * (n1 * bb + k1)) % n
    out = []
    for m, sign in ((mf, -1.0), (mi, 1.0)):
        ang = m.astype(F32) * (2.0 * math.pi / n)
        c, s = jnp.cos(ang), sign * jnp.sin(ang)
        top = jnp.concatenate([c, -s], axis=2)
        bot = jnp.concatenate([s, c], axis=2)
        out.append(jnp.concatenate([top, bot], axis=1))
    return out


def _fft_a_kernel(z_ref, fa_ref, o_ref, *, n1):
    z = z_ref[0]
    zz = z.reshape(z.shape[0] * z.shape[1], z.shape[2]).astype(BF16)
    out = _dot(fa_ref[...], zz)
    o_ref[0, 0] = out[:n1].astype(BF16)
    o_ref[0, 1] = out[n1:].astype(BF16)


def _fft_a(u, part, fa, n1, pairs):
    _, g, h, w = u.shape
    tl = _tile(w, 2048, LANES)
    fa_b = fa[:, :pairs * h].astype(BF16)
    return pl.pallas_call(
        functools.partial(_fft_a_kernel, n1=n1),
        grid=(g // pairs, w // tl),
        in_specs=[pl.BlockSpec((1, pairs, h, tl), lambda p, j: (part, p, 0, j)),
                  pl.BlockSpec((2 * n1, pairs * h), lambda p, j: (0, 0))],
        out_specs=pl.BlockSpec((1, 2, n1, tl), lambda p, j: (p, 0, 0, j)),
        out_shape=jax.ShapeDtypeStruct((g // pairs, 2, n1, w), BF16),
        compiler_params=_params("parallel", "parallel"),
        name="fft_first",
    )(u, fa_b)


def _fft_spec_kernel(b_ref, tf_ref, s_ref, o_ref, *, n1):
    x = jnp.concatenate([b_ref[0, 0, 0], b_ref[0, 1, 0]], axis=0).astype(BF16)
    a = _dot(tf_ref[0], x) * (1.0 / s_ref[...])
    o_ref[0, 0] = a[:n1].astype(BF16)
    o_ref[1, 0] = a[n1:].astype(BF16)


def _fft_spec(bc, tf, norm, n1, dcols):
    bc5 = bc.reshape(1, 2, n1, n1, dcols)
    return pl.pallas_call(
        functools.partial(_fft_spec_kernel, n1=n1),
        grid=(n1,),
        in_specs=[pl.BlockSpec((1, 2, 1, n1, dcols), lambda k: (0, 0, k, 0, 0)),
                  pl.BlockSpec((1, 2 * n1, 2 * n1), lambda k: (k, 0, 0)),
                  pl.BlockSpec((1, dcols), lambda k: (0, 0))],
        out_specs=pl.BlockSpec((2, 1, n1, dcols), lambda k: (0, k, 0, 0)),
        out_shape=jax.ShapeDtypeStruct((2, n1, n1, dcols), BF16),
        compiler_params=_params("parallel"),
        name="fft_filter_spectrum",
    )(bc5, tf, norm)


def _fft_mid_kernel(b_ref, tf_ref, ti_ref, f_ref, o_ref, *, n1):
    x = jnp.concatenate([b_ref[0, 0, 0], b_ref[0, 1, 0]], axis=0).astype(BF16)
    a = _dot(tf_ref[0], x)
    ar, ai = a[:n1], a[n1:]
    fr, fi = f_ref[0, 0].astype(F32), f_ref[1, 0].astype(F32)
    y = jnp.concatenate([ar * fr - ai * fi, ar * fi + ai * fr], axis=0).astype(BF16)
    gq = _dot(ti_ref[0], y)
    o_ref[0, 0, 0] = gq[:n1].astype(BF16)
    o_ref[0, 1, 0] = gq[n1:].astype(BF16)


def _fft_mid(bc, tf, ti, spec, order, n1, d):
    npair = bc.shape[0]
    bc5 = bc.reshape(npair, 2, n1, n1, d)
    return pl.pallas_call(
        functools.partial(_fft_mid_kernel, n1=n1),
        grid=(npair, n1),
        in_specs=[pl.BlockSpec((1, 2, 1, n1, d), lambda p, k: (p, 0, k, 0, 0)),
                  pl.BlockSpec((1, 2 * n1, 2 * n1), lambda p, k: (k, 0, 0)),
                  pl.BlockSpec((1, 2 * n1, 2 * n1), lambda p, k: (k, 0, 0)),
                  pl.BlockSpec((2, 1, n1, d), lambda p, k: (0, k, 0, order))],
        out_specs=pl.BlockSpec((1, 2, 1, n1, d), lambda p, k: (p, 0, k, 0, 0)),
        out_shape=jax.ShapeDtypeStruct((npair, 2, n1, n1, d), BF16),
        compiler_params=_params("parallel", "parallel"),
        name="fft_mid",
    )(bc5, tf, ti, spec)


def _fft_c_kernel(g_ref, fc_ref, u_ref, x_ref, sk_ref, o_ref, *, h):
    x = jnp.concatenate([g_ref[0, 0], g_ref[0, 1]], axis=0).astype(BF16)
    y = _dot(fc_ref[...], x)
    reps = u_ref.shape[3] // sk_ref.shape[1]
    skip = jnp.concatenate([sk_ref[...]] * reps, axis=1)
    o_ref[0, 0] = x_ref[0, 0].astype(F32) * (y[:h] + u_ref[0, 0].astype(F32) * skip)
    o_ref[0, 1] = x_ref[0, 1].astype(F32) * (y[h:] + u_ref[0, 1].astype(F32) * skip)


def _fft_c(gc, fc, u, upart, xg, xpart, skip, n1, d):
    npair = gc.shape[0]
    _, g, h, w = u.shape
    gc4 = gc.reshape(npair, 2, n1, w)
    tl = _tile(w, 2048, d)
    return pl.pallas_call(
        functools.partial(_fft_c_kernel, h=h),
        grid=(npair, w // tl),
        in_specs=[pl.BlockSpec((1, 2, n1, tl), lambda p, j: (p, 0, 0, j)),
                  pl.BlockSpec((2 * h, 2 * n1), lambda p, j: (0, 0)),
                  pl.BlockSpec((1, 2, h, tl), lambda p, j: (upart, p, 0, j)),
                  pl.BlockSpec((1, 2, h, tl), lambda p, j: (xpart, p, 0, j)),
                  pl.BlockSpec((1, d), lambda p, j: (0, 0))],
        out_specs=pl.BlockSpec((1, 2, h, tl), lambda p, j: (0, p, 0, j)),
        out_shape=jax.ShapeDtypeStruct((1, g, h, w), F32),
        compiler_params=_params("parallel", "parallel"),
        name="fft_last",
    )(gc4, fc.astype(BF16), u, xg, skip.reshape(1, d))


def _pad_cols(w, n):
    return jnp.pad(w, ((0, 0), (0, n - w.shape[1])))


def _gdn_layer(xa, modl, g1, lc, w_in, conv_w, a_log, dt_bias, norm_g, w_out):
    d = xa.shape[2]
    dk = 128
    heads = d // dk
    qk = heads * dk
    n_main = 4 * qk
    p, sc = _in_proj(xa, modl, g1, w_in[:, :n_main].astype(BF16), lc, _pad_cols(w_in[:, n_main:], LANES).astype(BF16))
    qkv = _gdn_prep(p, conv_w, lc, qk, dk)
    bg = _gdn_gates(sc, a_log, dt_bias, 0)
    wq, kd, u, oi, dg = _gdn_intra(qkv, bg, heads, dk)
    of, ob = _gdn_scan(wq, kd, u, dg, oi, lc, heads, dk)
    return _out_proj(xa, of, ob, p, 3, jnp.tile(norm_g, heads), w_out.astype(BF16), modl, lc, heads, dk, False)


def _ret_layer(xa, modl, g1, lc, w_in, norm_g, w_out):
    d = xa.shape[2]
    heads, dk, dv = d // 256, 256, 512
    p = _in_proj(xa, modl, g1, w_in.astype(BF16), lc)
    cos, sin = _axial_tables(xa.shape[1] - lc, lc, dk)
    log_gamma = jnp.log1p(-jnp.exp2(-5.0 - jnp.arange(heads, dtype=F32)))
    ld = jnp.stack([log_gamma, log_gamma[::-1]])
    ld = jnp.broadcast_to(ld[:, :, None, None], (2, heads, 1, dk))
    of, ob = _ret_scan(p, cos, sin, ld, lc, heads, dk, dv, heads)
    return _out_proj(xa, of, ob, p, 2, norm_g.reshape(-1), w_out.astype(BF16), modl, lc, heads, dv, True)


def _gla_layer(xa, modl, g1, lc, w_in, gate_w2, gate_b, norm_g, w_out):
    d = xa.shape[2]
    heads = 4
    dk, dv = d // 2 // heads, d // heads
    rank = gate_w2.shape[1]
    n_main = 2 * heads * dk + 2 * heads * dv
    p, low = _in_proj(xa, modl, g1, w_in[:, :n_main].astype(BF16), lc, _pad_cols(w_in[:, n_main:], LANES).astype(BF16))
    gw = jnp.zeros((2, LANES, heads * dk), F32)
    for dr in range(2):
        gw = gw.at[dr, dr * rank:(dr + 1) * rank, :].set(gate_w2[dr])
    gb = gate_b.reshape(2, 1, heads * dk)
    of, ob = _gla_scan(p, low, gw.astype(BF16), gb, lc, heads, dk, dv)
    return _out_proj(xa, of, ob, p, 2, jnp.tile(norm_g, heads), w_out.astype(BF16), modl, lc, heads, dv, False)


def _hyena_layer(xa, modl, g1, lc, w_in, conv_w, w1, b1, w2, b2, w3, fr, skip, w_out):
    b, lt, d = xa.shape
    l = lt - lc
    assert b % 2 == 0
    n1, fa, fc = _dft_consts(l)
    h = n1 // 2
    tf, ti = _mid_tables(n1)
    tf, ti = tf.astype(BF16), ti.astype(BF16)
    order = w3.shape[1] // d

    filt, norm = _hy_filter(l, w1, b1, w2, b2, w3, fr, d)
    fb = _fft_a(filt.reshape(1, 1, h, n1 * order * d), 0, fa, n1, 1)
    spec = _fft_spec(fb, tf, norm, n1, order * d)

    p = _in_proj(xa, modl, g1, w_in.astype(BF16), lc)
    vx = _hy_conv(p, conv_w, lc, d).reshape(3, b, h, n1 * d)
    u, upart = vx, 0
    for o in range(order):
        bc = _fft_a(u, upart, fa, n1, 2)
        gq = _fft_mid(bc, tf, ti, spec, o, n1, d)
        u, upart = _fft_c(gq, fc, u, upart, vx, 1 + o, skip[o], n1, d), 0
    return _plain_out(xa, u.reshape(b, l, d), w_out.astype(BF16), modl, lc)


def kernel(x, c, ctx, c_ctx, ada_w, ada_b, norm1_g, norm2_g, ffn_w1, ffn_w3, ffn_w2, gdn_w_in, gdn_conv_w, gdn_a_log, gdn_dt_bias, gdn_norm_g, gdn_w_out, ret_w_in, ret_norm_g, ret_w_out, gla_w_in, gla_gate_w2, gla_gate_b, gla_norm_g, gla_w_out, hy_w_in, hy_conv_w, hy_ff_w1, hy_ff_b1, hy_ff_w2, hy_ff_b2, hy_ff_w3, hy_sin_freq, hy_skip, hy_w_out, final_norm_g):
    b, l, d = x.shape
    lc = ctx.shape[1]
    depth = ada_w.shape[0]
    assert b < SUBLANES and lc % CHUNK == 0 and l % CHUNK == 0
    xa = jnp.concatenate([ctx, x], axis=1)
    cvec = jnp.zeros((SUBLANES, d), F32).at[:b].set(c).at[b].set(c_ctx)
    mod = _adaln(cvec, ada_w, ada_b).reshape(depth, SUBLANES, 6, d)
    for i in range(depth):
        kind, j = i % N_MIXERS, i // N_MIXERS
        modl, g1 = mod[i], norm1_g[i]
        if kind == 0:
            xa = _gdn_layer(xa, modl, g1, lc, gdn_w_in[j], gdn_conv_w[j], gdn_a_log[j], gdn_dt_bias[j],
                            gdn_norm_g[j], gdn_w_out[j])
        elif kind == 1:
            xa = _ret_layer(xa, modl, g1, lc, ret_w_in[j], ret_norm_g[j], ret_w_out[j])
        elif kind == 2:
            xa = _gla_layer(xa, modl, g1, lc, gla_w_in[j], gla_gate_w2[j], gla_gate_b[j], gla_norm_g[j],
                            gla_w_out[j])
        else:
            xa = _hyena_layer(xa, modl, g1, lc, hy_w_in[j], hy_conv_w[j], hy_ff_w1[j], hy_ff_b1[j], hy_ff_w2[j],
                              hy_ff_b2[j], hy_ff_w3[j], hy_sin_freq[j], hy_skip[j], hy_w_out[j])
        xa = _ffn(xa, modl, norm2_g[i], ffn_w1[i].astype(BF16), ffn_w3[i].astype(BF16), ffn_w2[i].astype(BF16), lc)
    return _final_norm(xa, final_norm_g, lc)
```
